```python
import jax, jax.numpy as jnp
from jax import lax
import numpy as np

D_MODEL = 1024
BATCH = 8
SEQ = 4096
DEPTH = 4

CHUNK = 64
N_MEM = 256
NORM_EPS = 1e-6
NEG_BIG = -1e30
GATE_CLIP = 60.0
MIX_WIDTH = D_MODEL
POOL_WIDTH = MIX_WIDTH // 2
POOL_WINDOWS = (2, 4, 8, 16)
POOL_GROUPS = len(POOL_WINDOWS)
POOL_GROUP_DIM = POOL_WIDTH // POOL_GROUPS
REC_WIDTH = MIX_WIDTH - POOL_WIDTH
REC_EXPAND = 128
REC_HEADS = REC_WIDTH // REC_EXPAND
REC_HEAD_DIM = REC_WIDTH // REC_HEADS
ATT_HEADS = 16
ATT_HEAD_DIM = MIX_WIDTH // ATT_HEADS
BAND_CHUNKS = 8
BAND = (BAND_CHUNKS + 1) * CHUNK
REL_CLIP = 256
MEM_HEADS = 4
MEM_HEAD_DIM = D_MODEL // MEM_HEADS
FF_DIM = 4 * D_MODEL
N_EVEN = (DEPTH + 1) // 2
N_ODD = DEPTH // 2

kernel_name = 'hybrid_pool_hgrn2_chunkattn_trunk'


def rmsnorm(x, g):
    xf = x.astype(jnp.float32)
    y = xf * lax.rsqrt(jnp.mean(xf * xf, axis=-1, keepdims=True) + NORM_EPS)
    return y.astype(x.dtype) * g


def causal_multiscale_pool(u):
    S_ = u.shape[1]
    uf = u.astype(jnp.float32)
    cs = jnp.cumsum(uf, axis=1)
    pos = jnp.arange(1, S_ + 1, dtype=jnp.float32)
    outs = []
    for g, w in enumerate(POOL_WINDOWS):
        c = cs[:, :, g]
        lag = jnp.pad(c, ((0, 0), (w, 0), (0, 0)))[:, :S_]
        outs.append((c - lag) / jnp.minimum(pos, w)[:, None])
    mean = jnp.stack(outs, axis=2)
    return (mean - uf).astype(u.dtype)


def hgrn2_chunkwise(q, k, v, logf):
    B_, S_, H, dk = q.shape
    dv = v.shape[-1]
    N = S_ // CHUNK

    def to_chunks(a):
        return a.reshape(B_, N, CHUNK, H, a.shape[-1]).transpose(1, 0, 3, 2, 4)

    causal = jnp.tril(jnp.ones((CHUNK, CHUNK), dtype=bool))[None, None, :, :, None]

    def step(state, inp):
        qc, kc, vc, lc = inp
        b = jnp.cumsum(lc, axis=2)
        o_inter = jnp.einsum('bhtd,bhde->bhte', qc * jnp.exp(b), state)
        diff = b[:, :, :, None, :] - b[:, :, None, :, :]
        decay = jnp.where(causal, jnp.exp(jnp.where(causal, diff, 0.0)), 0.0)
        scores = jnp.einsum('bhtd,bhsd,bhtsd->bhts', qc, kc, decay)
        o = o_inter + jnp.einsum('bhts,bhse->bhte', scores, vc)
        b_last = b[:, :, -1:, :]
        new_state = jnp.exp(b_last[:, :, 0, :])[..., None] * state + jnp.einsum(
            'bhsd,bhse->bhde', kc * jnp.exp(b_last - b), vc)
        return new_state, o

    init = jnp.zeros((B_, H, dk, dv), jnp.float32)
    _, o = lax.scan(step, init, (to_chunks(q), to_chunks(k), to_chunks(v), to_chunks(logf)))
    return o.transpose(1, 0, 3, 2, 4).reshape(B_, S_, H, dv)


def even_mixer(h, w_in, pool_w, pool_scale, lb, out_gain, w_out):
    B_, S_, _ = h.shape
    z = h @ w_in
    cuts = [POOL_WIDTH + r * REC_WIDTH for r in range(4)]
    u, q, fz, i, g = jnp.split(z, cuts, axis=-1)
    pooled = causal_multiscale_pool(u.reshape(B_, S_, POOL_GROUPS, POOL_GROUP_DIM))
    a_out = jnp.einsum('bsgc,gcd->bsgd', pooled, pool_w).reshape(B_, S_, POOL_WIDTH) * pool_scale
    fz32 = jnp.clip(fz.astype(jnp.float32), -GATE_CLIP, GATE_CLIP)
    logf = jax.nn.log_sigmoid(fz32) + jnp.log1p(lb * jnp.exp(-fz32))
    k = (1.0 - lb) * jax.nn.sigmoid(-fz32)
    hs = lambda a: a.reshape(B_, S_, REC_HEADS, -1)
    o = hgrn2_chunkwise(hs(q.astype(jnp.float32)), hs(k), hs(i.astype(jnp.float32)), hs(logf))
    o = rmsnorm(o, out_gain.astype(jnp.float32)) * jax.nn.silu(hs(g.astype(jnp.float32)))
    b_out = o.reshape(B_, S_, REC_WIDTH).astype(h.dtype)
    return jnp.concatenate([a_out.astype(h.dtype), b_out], axis=-1) @ w_out


def chunk_band_attention(q, k, v, rel_bias):
    B_, S_, H, dh = q.shape
    N = S_ // CHUNK
    pad = ((0, 0), (BAND_CHUNKS * CHUNK, 0), (0, 0), (0, 0))
    kp = jnp.pad(k, pad)
    vp = jnp.pad(v, pad)
    t = jnp.arange(CHUNK)[:, None]
    j = jnp.arange(BAND)[None, :]
    rel = BAND_CHUNKS * CHUNK + t - j
    idx = jnp.clip(rel, -REL_CLIP, REL_CLIP) + REL_CLIP
    bias = rel_bias[:, idx].astype(jnp.float32)
    scale = dh ** -0.5

    def per_chunk(n):
        qn = lax.dynamic_slice_in_dim(q, n * CHUNK, CHUNK, axis=1)
        kn = lax.dynamic_slice_in_dim(kp, n * CHUNK, BAND, axis=1)
        vn = lax.dynamic_slice_in_dim(vp, n * CHUNK, BAND, axis=1)
        s = jnp.einsum('bthd,bjhd->bhtj', qn, kn).astype(jnp.float32) * scale + bias
        valid = (n - BAND_CHUNKS) * CHUNK + jnp.arange(BAND) >= 0
        s = jnp.where(valid[None, None, None, :], s, NEG_BIG)
        p = jax.nn.softmax(s, axis=-1).astype(v.dtype)
        return jnp.einsum('bhtj,bjhd->bthd', p, vn)

    out = lax.map(per_chunk, jnp.arange(N))
    return jnp.moveaxis(out, 0, 1).reshape(B_, S_, H, dh)


def odd_mixer(h, w_qkv, rel_bias, w_o):
    B_, S_, _ = h.shape
    qkv = (h @ w_qkv).reshape(B_, S_, 3, ATT_HEADS, ATT_HEAD_DIM)
    o = chunk_band_attention(qkv[:, :, 0], qkv[:, :, 1], qkv[:, :, 2], rel_bias)
    return o.reshape(B_, S_, MIX_WIDTH) @ w_o


def memory_cross_attention(h, mem_n, w_q, w_kv, w_o):
    B_, S_, _ = h.shape
    M = mem_n.shape[1]
    q = (h @ w_q).reshape(B_, S_, MEM_HEADS, MEM_HEAD_DIM)
    kv = (mem_n @ w_kv).reshape(B_, M, 2, MEM_HEADS, MEM_HEAD_DIM)
    s = jnp.einsum('bshd,bmhd->bhsm', q, kv[:, :, 0]).astype(jnp.float32) * MEM_HEAD_DIM ** -0.5
    p = jax.nn.softmax(s, axis=-1).astype(h.dtype)
    o = jnp.einsum('bhsm,bmhd->bshd', p, kv[:, :, 1]).reshape(B_, S_, D_MODEL)
    return o @ w_o


def squared_relu_mlp(h, w1, w2):
    a = jax.nn.relu(h @ w1)
    return (a * a) @ w2


def setup_inputs(seed: int = 0) -> dict:
    key = jax.random.key(seed)
    ks = jax.random.split(key, 20)
    f32 = jnp.float32

    def dense(k, shape):
        return jax.random.normal(k, shape, f32) * shape[-2] ** -0.5

    return {
        'x': jax.random.normal(ks[0], (BATCH, SEQ, D_MODEL), f32),
        'mem': jax.random.normal(ks[1], (BATCH, N_MEM, D_MODEL), f32),
        'norm_gains': 1.0 + 0.05 * jax.random.normal(ks[2], (DEPTH, 6, D_MODEL), f32),
        'mem_norm_gain': 1.0 + 0.05 * jax.random.normal(ks[3], (D_MODEL,), f32),
        'w_in_ab': dense(ks[4], (N_EVEN, D_MODEL, POOL_WIDTH + 4 * REC_WIDTH)),
        'pool_w': dense(ks[5], (N_EVEN, POOL_GROUPS, POOL_GROUP_DIM, POOL_GROUP_DIM)),
        'pool_scale': 1.0 + 0.1 * jax.random.normal(ks[6], (N_EVEN, POOL_WIDTH), f32),
        'rec_lb_logits': 0.1 * jax.random.normal(ks[7], (N_EVEN, REC_WIDTH), f32),
        'rec_out_gain': 1.0 + 0.05 * jax.random.normal(ks[8], (N_EVEN, REC_HEAD_DIM), f32),
        'w_out_ab': dense(ks[9], (N_EVEN, MIX_WIDTH, D_MODEL)),
        'w_qkv': dense(ks[10], (N_ODD, D_MODEL, 3 * MIX_WIDTH)),
        'rel_bias': 0.1 * jax.random.normal(ks[11], (N_ODD, ATT_HEADS, 2 * REL_CLIP + 1), f32),
        'w_o_att': dense(ks[12], (N_ODD, MIX_WIDTH, D_MODEL)),
        'w_mem_q': dense(ks[13], (DEPTH, D_MODEL, D_MODEL)),
        'w_mem_kv': dense(ks[14], (DEPTH, D_MODEL, 2 * D_MODEL)),
        'w_mem_o': dense(ks[15], (DEPTH, D_MODEL, D_MODEL)),
        'w_ff1': dense(ks[16], (DEPTH, D_MODEL, FF_DIM)),
        'w_ff2': dense(ks[17], (DEPTH, FF_DIM, D_MODEL)),
    }


def reference(x, mem, norm_gains, mem_norm_gain, w_in_ab, pool_w, pool_scale, rec_lb_logits,
              rec_out_gain, w_out_ab, w_qkv, rel_bias, w_o_att, w_mem_q, w_mem_kv, w_mem_o,
              w_ff1, w_ff2):
    lb_p = jax.nn.softmax(rec_lb_logits.astype(jnp.float32), axis=0)
    lbs = jnp.clip(jnp.cumsum(lb_p, axis=0) - lb_p[0], 0.0, 1.0)
    mem_n = rmsnorm(mem, mem_norm_gain)
    for l in range(DEPTH):
        g = norm_gains[l]
        h = rmsnorm(x, g[0])
        if l % 2 == 0:
            e = l // 2
            y = even_mixer(h, w_in_ab[e], pool_w[e], pool_scale[e], lbs[e], rec_out_gain[e],
                           w_out_ab[e])
        else:
            o = l // 2
            y = odd_mixer(h, w_qkv[o], rel_bias[o], w_o_att[o])
        x = x + rmsnorm(y, g[1])
        h = rmsnorm(x, g[2])
        x = x + rmsnorm(memory_cross_attention(h, mem_n, w_mem_q[l], w_mem_kv[l], w_mem_o[l]), g[3])
        h = rmsnorm(x, g[4])
        x = x + rmsnorm(squared_relu_mlp(h, w_ff1[l], w_ff2[l]), g[5])
    return x
```

```python
import functools

import jax
import jax.numpy as jnp
from jax import lax
from jax.experimental import pallas as pl
from jax.experimental.pallas import tpu as pltpu

NORM_EPS = 1e-6
NEG_BIG = -1e30
GATE_CLIP = 60.0
CHUNK = 64
POOL_WINDOWS = (2, 4, 8, 16)
POOL_GROUP_DIM = 128
POOL_WIDTH = 512
REC_WIDTH = 512
REC_HEADS = 4
REC_HEAD_DIM = 128
ATT_HEADS = 16
ATT_HEAD_DIM = 64
BAND_CHUNKS = 8
REL_CLIP = 256
MEM_HEADS = 4

LANES = 128
MIX_TILE = 256
POOL_HALO = 16
BAND_ROWS = BAND_CHUNKS * CHUNK
VMEM_LIMIT = 56 * 1024 * 1024

BF16 = jnp.bfloat16
F32 = jnp.float32


def _rmsnorm(x, g):
    return x * lax.rsqrt(jnp.mean(x * x, axis=-1, keepdims=True) + NORM_EPS) * g


def _dot(a, b):
    return jnp.dot(a, b, preferred_element_type=F32)


def _dot_nt(a, b):
    return lax.dot_general(a, b, (((1,), (1,)), ((), ())), preferred_element_type=F32)


def _dot_tn(a, b):
    return lax.dot_general(a, b, (((0,), (0,)), ((), ())), preferred_element_type=F32)


def _const_spec(shape):
    nd = len(shape)
    return pl.BlockSpec(shape, lambda *_: (0,) * nd, pipeline_mode=pl.Buffered(1))


def _tile_spec(tile, d):
    return pl.BlockSpec((1, tile, d), lambda b, j: (b, j, 0))


def _params(sem):
    return pltpu.CompilerParams(dimension_semantics=sem, vmem_limit_bytes=VMEM_LIMIT)


def _even_kernel(x_ref, g0_ref, g1_ref, w_in_ref, pool_w_ref, pool_scale_ref, lb_ref,
                 out_gain_ref, w_out_ref, o_ref, state_ref, tail_ref):
    tile = x_ref.shape[1]
    j = pl.program_id(1)

    @pl.when(j == 0)
    def _():
        state_ref[...] = jnp.zeros_like(state_ref)
        tail_ref[...] = jnp.zeros_like(tail_ref)

    x = x_ref[0]
    h = _rmsnorm(x, g0_ref[...]).astype(BF16)
    z = _dot(h, w_in_ref[...])
    u = z[:, :POOL_WIDTH]

    cat = jnp.concatenate([tail_ref[...], u], axis=0)
    tail_ref[...] = u[tile - POOL_HALO:, :]
    row = lax.broadcasted_iota(jnp.int32, (tile, 1), 0)
    pos = (j * tile + row + 1).astype(F32)
    outs = []
    for g, w in enumerate(POOL_WINDOWS):
        sl = slice(g * POOL_GROUP_DIM, (g + 1) * POOL_GROUP_DIM)
        s = cat[:, sl]
        for k in range(g + 1):
            s = s + pltpu.roll(s, 1 << k, axis=0)
        mean = s[POOL_HALO:, :] / jnp.minimum(pos, float(w))
        pooled = (mean - u[:, sl]).astype(BF16)
        outs.append(_dot(pooled, pool_w_ref[g]))
    a_out = jnp.concatenate(outs, axis=-1) * pool_scale_ref[...]

    rows = lax.broadcasted_iota(jnp.int32, (tile, LANES), 0)
    tt = lax.broadcasted_iota(jnp.int32, (tile, tile), 0)
    ss = lax.broadcasted_iota(jnp.int32, (tile, tile), 1)
    split = tt ^ ss
    b_outs = []
    for hh in range(REC_HEADS):
        def col(r, hh=hh):
            base = POOL_WIDTH + r * REC_WIDTH + hh * REC_HEAD_DIM
            return z[:, base:base + REC_HEAD_DIM]
        q, fz, v, gate = col(0), col(1), col(2), col(3)
        lb = lb_ref[:, hh * REC_HEAD_DIM:(hh + 1) * REC_HEAD_DIM]
        a = jnp.exp(-jnp.clip(fz, -GATE_CLIP, GATE_CLIP))
        logf = jnp.log1p(lb * a) - jnp.log1p(a)
        kk = (1.0 - lb) * (a / (1.0 + a))
        vb = v.astype(BF16)

        scores = jnp.where(split == 0, _dot_nt(q.astype(BF16), kk.astype(BF16)), 0.0)
        c = logf
        tot = logf
        half = 1
        while half < tile:
            upper = (rows & half) != 0
            e = jnp.exp(jnp.where(upper, c, tot - c))
            qs = jnp.where(upper, q * e, 0.0).astype(BF16)
            ks = jnp.where(upper, 0.0, kk * e).astype(BF16)
            level = (split >> (half.bit_length() - 1)) == 1
            scores = jnp.where(level, _dot_nt(qs, ks), scores)
            dn = pltpu.roll(tot, half, axis=0)
            up = pltpu.roll(tot, tile - half, axis=0)
            c = c + jnp.where(upper, dn, 0.0)
            tot = tot + jnp.where(upper, dn, up)
            half *= 2
        st = state_ref[hh]
        o = _dot(scores.astype(BF16), vb) + _dot_nt((q * jnp.exp(c)).astype(BF16), st.astype(BF16))
        k_end = (kk * jnp.exp(tot - c)).astype(BF16)
        state_ref[hh] = st * jnp.exp(tot[:1, :]) + _dot_tn(vb, k_end)
        o = _rmsnorm(o, out_gain_ref[...])
        b_outs.append(o * (gate * jax.nn.sigmoid(gate)))
    mixed = jnp.concatenate([a_out] + b_outs, axis=-1).astype(BF16)
    y = _dot(mixed, w_out_ref[...])
    o_ref[0] = x + _rmsnorm(y, g1_ref[...])


def _even_mixer(x, g0, g1, w_in, pool_w, pool_scale, lb, out_gain, w_out):
    b, s, d = x.shape
    tile = MIX_TILE
    return pl.pallas_call(
        _even_kernel,
        grid=(b, s // tile),
        in_specs=[
            _tile_spec(tile, d),
            _const_spec(g0.shape), _const_spec(g1.shape), _const_spec(w_in.shape),
            _const_spec(pool_w.shape), _const_spec(pool_scale.shape), _const_spec(lb.shape),
            _const_spec(out_gain.shape), _const_spec(w_out.shape),
        ],
        out_specs=_tile_spec(tile, d),
        out_shape=jax.ShapeDtypeStruct(x.shape, x.dtype),
        scratch_shapes=[
            pltpu.VMEM((REC_HEADS, REC_HEAD_DIM, REC_HEAD_DIM), F32),
            pltpu.VMEM((POOL_HALO, POOL_WIDTH), F32),
        ],
        compiler_params=_params(("arbitrary", "arbitrary")),
        name="even_mixer",
    )(x, g0, g1, w_in, pool_w, pool_scale, lb, out_gain, w_out)


def _odd_kernel(x_ref, g0_ref, g1_ref, w_qkv_ref, bias_ref, w_o_ref, o_ref, k_ref, v_ref):
    tile = x_ref.shape[1]
    d = x_ref.shape[2]
    keys = BAND_ROWS + tile
    j = pl.program_id(1)

    @pl.when(j == 0)
    def _():
        k_ref[:BAND_ROWS, :] = jnp.zeros((BAND_ROWS, d), BF16)
        v_ref[:BAND_ROWS, :] = jnp.zeros((BAND_ROWS, d), BF16)

    x = x_ref[0]
    h = _rmsnorm(x, g0_ref[...]).astype(BF16)
    qkv = _dot(h, w_qkv_ref[...])
    q = (qkv[:, :d] * (ATT_HEAD_DIM ** -0.5)).astype(BF16)
    k_ref[BAND_ROWS:, :] = qkv[:, d:2 * d].astype(BF16)
    v_ref[BAND_ROWS:, :] = qkv[:, 2 * d:].astype(BF16)

    key_pos = j * tile - BAND_ROWS + lax.broadcasted_iota(jnp.int32, (1, keys), 1)
    start_mask = jnp.where(key_pos >= 0, 0.0, NEG_BIG)
    low_half = lax.broadcasted_iota(jnp.int32, (1, LANES), 1) < ATT_HEAD_DIM
    outs = []
    for p in range(ATT_HEADS // 2):
        sl = slice(p * LANES, (p + 1) * LANES)
        qp, kp, vp = q[:, sl], k_ref[:, sl], v_ref[:, sl]
        pair = []
        for e in range(2):
            own = low_half if e == 0 else jnp.logical_not(low_half)
            qm = jnp.where(own, qp, jnp.zeros_like(qp))
            s = _dot_nt(qm, kp) + bias_ref[2 * p + e] + start_mask
            m = jnp.max(s, axis=-1, keepdims=True)
            pr = jnp.exp(s - m)
            l = jnp.sum(pr, axis=-1, keepdims=True)
            pair.append(_dot(pr.astype(BF16), vp) / l)
        outs.append(jnp.where(low_half, pair[0], pair[1]))
    att = jnp.concatenate(outs, axis=-1).astype(BF16)
    y = _dot(att, w_o_ref[...])
    o_ref[0] = x + _rmsnorm(y, g1_ref[...])

    for ref in (k_ref, v_ref):
        for r in range(0, BAND_ROWS, tile):
            ref[r:r + tile, :] = ref[r + tile:r + 2 * tile, :]


def _band_bias(rel_bias, tile):
    r = jnp.arange(tile)[:, None]
    c = jnp.arange(BAND_ROWS + tile)[None, :]
    rel = BAND_ROWS + r - c
    idx = jnp.clip(rel, -REL_CLIP, REL_CLIP) + REL_CLIP
    lag = (r // CHUNK + BAND_CHUNKS) - c // CHUNK
    in_band = (lag >= 0) & (lag <= BAND_CHUNKS)
    return jnp.where(in_band[None], rel_bias[:, idx].astype(F32), NEG_BIG)


def _odd_mixer(x, g0, g1, w_qkv, bias, w_o):
    b, s, d = x.shape
    tile = MIX_TILE
    return pl.pallas_call(
        _odd_kernel,
        grid=(b, s // tile),
        in_specs=[
            _tile_spec(tile, d),
            _const_spec(g0.shape), _const_spec(g1.shape), _const_spec(w_qkv.shape),
            _const_spec(bias.shape), _const_spec(w_o.shape),
        ],
        out_specs=_tile_spec(tile, d),
        out_shape=jax.ShapeDtypeStruct(x.shape, x.dtype),
        scratch_shapes=[
            pltpu.VMEM((BAND_ROWS + tile, d), BF16),
            pltpu.VMEM((BAND_ROWS + tile, d), BF16),
        ],
        compiler_params=_params(("arbitrary", "arbitrary")),
        name="odd_mixer",
    )(x, g0, g1, w_qkv, bias, w_o)


def _mem_kv_kernel(mem_ref, g_ref, w_kv_ref, kv_ref):
    mem_n = _rmsnorm(mem_ref[0], g_ref[...]).astype(BF16)
    kv_ref[0, 0] = _dot(mem_n, w_kv_ref[0]).astype(BF16)


def _mem_kv(mem, gain, w_kv):
    b, m, d = mem.shape
    depth = w_kv.shape[0]
    return pl.pallas_call(
        _mem_kv_kernel,
        grid=(depth, b),
        in_specs=[
            pl.BlockSpec((1, m, d), lambda l, i: (i, 0, 0)),
            pl.BlockSpec((1, d), lambda l, i: (0, 0)),
            pl.BlockSpec((1, d, 2 * d), lambda l, i: (l, 0, 0)),
        ],
        out_specs=pl.BlockSpec((1, 1, m, 2 * d), lambda l, i: (l, i, 0, 0)),
        out_shape=jax.ShapeDtypeStruct((depth, b, m, 2 * d), BF16),
        compiler_params=_params(("arbitrary", "arbitrary")),
        name="mem_kv",
    )(mem, gain, w_kv)


def _mem_attn_kernel(x_ref, g2_ref, g3_ref, w_q_ref, kv_ref, w_o_ref, o_ref):
    d = x_ref.shape[2]
    dh = d // MEM_HEADS
    x = x_ref[0]
    h = _rmsnorm(x, g2_ref[...]).astype(BF16)
    q = (_dot(h, w_q_ref[...]) * (dh ** -0.5)).astype(BF16)
    outs = []
    for hh in range(MEM_HEADS):
        sl = slice(hh * dh, (hh + 1) * dh)
        kh = kv_ref[0, :, sl]
        vh = kv_ref[0, :, d + hh * dh:d + (hh + 1) * dh]
        s = _dot_nt(q[:, sl], kh)
        m = jnp.max(s, axis=-1, keepdims=True)
        pr = jnp.exp(s - m)
        l = jnp.sum(pr, axis=-1, keepdims=True)
        outs.append(_dot(pr.astype(BF16), vh) / l)
    att = jnp.concatenate(outs, axis=-1).astype(BF16)
    y = _dot(att, w_o_ref[...])
    o_ref[0] = x + _rmsnorm(y, g3_ref[...])


def _mem_attn(x, g2, g3, w_q, kv, w_o, tile):
    b, s, d = x.shape
    m = kv.shape[1]
    return pl.pallas_call(
        _mem_attn_kernel,
        grid=(b, s // tile),
        in_specs=[
            _tile_spec(tile, d),
            _const_spec(g2.shape), _const_spec(g3.shape), _const_spec(w_q.shape),
            pl.BlockSpec((1, m, 2 * d), lambda i, j: (i, 0, 0)),
            _const_spec(w_o.shape),
        ],
        out_specs=_tile_spec(tile, d),
        out_shape=jax.ShapeDtypeStruct(x.shape, x.dtype),
        compiler_params=_params(("arbitrary", "arbitrary")),
        name="mem_attn",
    )(x, g2, g3, w_q, kv, w_o)


def _mlp_kernel(x_ref, g4_ref, g5_ref, w1_ref, w2_ref, o_ref):
    x = x_ref[0]
    h = _rmsnorm(x, g4_ref[...]).astype(BF16)
    a = jnp.maximum(_dot(h, w1_ref[...]), 0.0)
    y = _dot((a * a).astype(BF16), w2_ref[...])
    o_ref[0] = x + _rmsnorm(y, g5_ref[...])


def _mlp(x, g4, g5, w1, w2, tile):
    b, s, d = x.shape
    return pl.pallas_call(
        _mlp_kernel,
        grid=(b, s // tile),
        in_specs=[
            _tile_spec(tile, d),
            _const_spec(g4.shape), _const_spec(g5.shape), _const_spec(w1.shape),
            _const_spec(w2.shape),
        ],
        out_specs=_tile_spec(tile, d),
        out_shape=jax.ShapeDtypeStruct(x.shape, x.dtype),
        compiler_params=_params(("arbitrary", "arbitrary")),
        name="mlp",
    )(x, g4, g5, w1, w2)


def kernel(x, mem, norm_gains, mem_norm_gain, w_in_ab, pool_w, pool_scale, rec_lb_logits,
           rec_out_gain, w_out_ab, w_qkv, rel_bias, w_o_att, w_mem_q, w_mem_kv, w_mem_o,
           w_ff1, w_ff2):
    depth = norm_gains.shape[0]
    s = x.shape[1]
    dense_tile = min(512, s)
    row = lambda a: a.reshape(1, -1)
    lb_p = jax.nn.softmax(rec_lb_logits.astype(F32), axis=0)
    lbs = jnp.clip(jnp.cumsum(lb_p, axis=0) - lb_p[0], 0.0, 1.0)
    kv = _mem_kv(mem, row(mem_norm_gain), w_mem_kv.astype(BF16))
    for l in range(depth):
        g = norm_gains[l]
        if l % 2 == 0:
            e = l // 2
            x = _even_mixer(x, row(g[0]), row(g[1]), w_in_ab[e].astype(BF16),
                            pool_w[e].astype(BF16), row(pool_scale[e]), row(lbs[e]),
                            row(rec_out_gain[e]), w_out_ab[e].astype(BF16))
        else:
            o = l // 2
            x = _odd_mixer(x, row(g[0]), row(g[1]), w_qkv[o].astype(BF16),
                           _band_bias(rel_bias[o], MIX_TILE), w_o_att[o].astype(BF16))
        x = _mem_attn(x, row(g[2]), row(g[3]), w_mem_q[l].astype(BF16), kv[l],
                      w_mem_o[l].astype(BF16), dense_tile)
        x = _mlp(x, row(g[4]), row(g[5]), w_ff1[l].astype(BF16), w_ff2[l].astype(BF16),
                 dense_tile)
    return x
```

```python
import functools

import jax
import jax.numpy as jnp
from jax import lax
from jax.experimental import pallas as pl
from jax.experimental.pallas import tpu as pltpu

NORM_EPS = 1e-6
NEG_BIG = -1e30
GATE_CLIP = 60.0
CHUNK = 64
POOL_WINDOWS = (2, 4, 8, 16)
POOL_GROUP_DIM = 128
POOL_WIDTH = 512
REC_WIDTH = 512
REC_HEADS = 4
REC_HEAD_DIM = 128
ATT_HEADS = 16
ATT_HEAD_DIM = 64
BAND_CHUNKS = 8
REL_CLIP = 256
MEM_HEADS = 4

LANES = 128
SUBLANES = 8
MIX_TILE = 256
POOL_HALO = 16
BAND_ROWS = BAND_CHUNKS * CHUNK
SCORE_SLOTS = 6
VMEM_LIMIT = 56 * 1024 * 1024

BF16 = jnp.bfloat16
F32 = jnp.float32


def _rmsnorm(x, g):
    return x * lax.rsqrt(jnp.mean(x * x, axis=-1, keepdims=True) + NORM_EPS) * g


def _dot(a, b):
    return jnp.dot(a, b, preferred_element_type=F32)


def _dot_nt(a, b):
    return lax.dot_general(a, b, (((1,), (1,)), ((), ())), preferred_element_type=F32)


def _dot_tn(a, b):
    return lax.dot_general(a, b, (((0,), (0,)), ((), ())), preferred_element_type=F32)


def _const_spec(shape):
    nd = len(shape)
    return pl.BlockSpec(shape, lambda *_: (0,) * nd, pipeline_mode=pl.Buffered(1))


def _tile_spec(tile, d):
    return pl.BlockSpec((1, tile, d), lambda b, j: (b, j, 0))


def _params(sem, flags=None):
    return pltpu.CompilerParams(dimension_semantics=sem, vmem_limit_bytes=VMEM_LIMIT, flags=flags)


def _even_kernel(x_ref, g0_ref, g1_ref, w_in_ref, pool_w_ref, pool_scale_ref, lb_ref,
                 out_gain_ref, w_out_ref, o_ref, state_ref, tail_ref):
    tile = x_ref.shape[1]
    j = pl.program_id(1)

    @pl.when(j == 0)
    def _():
        state_ref[...] = jnp.zeros_like(state_ref)
        tail_ref[...] = jnp.zeros_like(tail_ref)

    x = x_ref[0]
    h = _rmsnorm(x, g0_ref[...]).astype(BF16)
    z = _dot(h, w_in_ref[...])
    u = z[:, :POOL_WIDTH]

    cat = jnp.concatenate([tail_ref[...], u], axis=0)
    tail_ref[...] = u[tile - POOL_HALO:, :]
    row = lax.broadcasted_iota(jnp.int32, (tile, 1), 0)
    pos = (j * tile + row + 1).astype(F32)
    outs = []
    for g, w in enumerate(POOL_WINDOWS):
        sl = slice(g * POOL_GROUP_DIM, (g + 1) * POOL_GROUP_DIM)
        s = cat[:, sl]
        for k in range(g + 1):
            s = s + pltpu.roll(s, 1 << k, axis=0)
        mean = s[POOL_HALO:, :] / jnp.minimum(pos, float(w))
        pooled = (mean - u[:, sl]).astype(BF16)
        outs.append(_dot(pooled, pool_w_ref[g]))
    a_out = jnp.concatenate(outs, axis=-1) * pool_scale_ref[...]

    rows = lax.broadcasted_iota(jnp.int32, (tile, LANES), 0)
    ri = lax.broadcasted_iota(jnp.int32, (SUBLANES, LANES), 0)
    ci = lax.broadcasted_iota(jnp.int32, (SUBLANES, LANES), 1)
    n_row, n_lane = tile // SUBLANES, tile // LANES
    split = [(ri + SUBLANES * o) ^ ci for o in range(LANES // SUBLANES)]
    cols = [ci + LANES * l for l in range(n_lane)]
    b_outs = []
    for hh in range(REC_HEADS):
        def col(r, hh=hh):
            base = POOL_WIDTH + r * REC_WIDTH + hh * REC_HEAD_DIM
            return z[:, base:base + REC_HEAD_DIM]
        q, fz, v, gate = col(0), col(1), col(2), col(3)
        lb = lb_ref[:, hh * REC_HEAD_DIM:(hh + 1) * REC_HEAD_DIM]
        a = jnp.exp(-jnp.clip(fz, -GATE_CLIP, GATE_CLIP))
        inv = 1.0 / (1.0 + a)
        logf = jnp.log((1.0 + lb * a) * inv)
        kk = (1.0 - lb) * (a * inv)
        vb = v.astype(BF16)

        def piece(p, r, l):
            return p[r * SUBLANES:(r + 1) * SUBLANES, l * LANES:(l + 1) * LANES]

        def diag_lane(r):
            return r * SUBLANES // LANES

        def diag_split(r):
            return split[r % (LANES // SUBLANES)]

        p = _dot_nt(q.astype(BF16), kk.astype(BF16))
        sc = [[jnp.where(diag_split(r) == 0, piece(p, r, l), 0.0) if l == diag_lane(r)
               else jnp.zeros((SUBLANES, LANES), F32) for l in range(n_lane)] for r in range(n_row)]

        c = logf
        tot = logf
        half = 1
        while half < SUBLANES:
            upper = (rows & half) != 0
            e = jnp.exp(jnp.where(upper, c, tot - c))
            qs = jnp.where(upper, q * e, 0.0).astype(BF16)
            ks = jnp.where(upper, 0.0, kk * e).astype(BF16)
            p = _dot_nt(qs, ks)
            for r in range(n_row):
                l = diag_lane(r)
                level = (diag_split(r) >> (half.bit_length() - 1)) == 1
                sc[r][l] = jnp.where(level, piece(p, r, l), sc[r][l])
            dn = pltpu.roll(tot, half, axis=0)
            up = pltpu.roll(tot, tile - half, axis=0)
            c = c + jnp.where(upper, dn, 0.0)
            tot = tot + jnp.where(upper, dn, up)
            half *= 2

        cb = [c[i * half:(i + 1) * half] for i in range(tile // half)]
        tb = [tot[i * half:i * half + 1] for i in range(tile // half)]
        while half < tile:
            n_blk = tile // half
            qs, ks = [], []
            for i in range(0, n_blk, 2):
                lo = slice(i * half, (i + 1) * half)
                hi = slice((i + 1) * half, (i + 2) * half)
                ks.append((kk[lo] * jnp.exp(tb[i] - cb[i])).astype(BF16))
                ks.append(jnp.zeros((half, LANES), BF16))
                qs.append((q[hi] * jnp.exp(cb[i + 1])).astype(BF16))
            p = _dot_nt(jnp.concatenate(qs, axis=0), jnp.concatenate(ks, axis=0))
            for pair in range(n_blk // 2):
                col0 = 2 * pair * half
                for a in range(half // SUBLANES):
                    r = (2 * pair + 1) * half // SUBLANES + a
                    for l in range(col0 // LANES, (col0 + half - 1) // LANES + 1):
                        new = piece(p, pair * half // SUBLANES + a, l)
                        if 2 * half < LANES:
                            same_pair = (cols[l] >> (2 * half).bit_length() - 1) == pair
                            new = jnp.where(same_pair, new, 0.0)
                        sc[r][l] = sc[r][l] + new
            cb = [jnp.concatenate([cb[i], cb[i + 1] + tb[i]], axis=0) for i in range(0, n_blk, 2)]
            tb = [tb[i] + tb[i + 1] for i in range(0, n_blk, 2)]
            half *= 2
        c, tot = cb[0], tb[0]
        scores = jnp.concatenate([jnp.concatenate(r, axis=1) for r in sc], axis=0)

        st = state_ref[hh]
        o = _dot(scores.astype(BF16), vb) + _dot_nt((q * jnp.exp(c)).astype(BF16), st.astype(BF16))
        k_end = (kk * jnp.exp(tot - c)).astype(BF16)
        state_ref[hh] = st * jnp.exp(tot) + _dot_tn(vb, k_end)
        o = _rmsnorm(o, out_gain_ref[...])
        b_outs.append(o * (gate * jax.nn.sigmoid(gate)))
    mixed = jnp.concatenate([a_out] + b_outs, axis=-1).astype(BF16)
    y = _dot(mixed, w_out_ref[...])
    o_ref[0] = x + _rmsnorm(y, g1_ref[...])


def _even_mixer(x, g0, g1, w_in, pool_w, pool_scale, lb, out_gain, w_out):
    b, s, d = x.shape
    tile = MIX_TILE
    return pl.pallas_call(
        _even_kernel,
        grid=(b, s // tile),
        in_specs=[
            _tile_spec(tile, d),
            _const_spec(g0.shape), _const_spec(g1.shape), _const_spec(w_in.shape),
            _const_spec(pool_w.shape), _const_spec(pool_scale.shape), _const_spec(lb.shape),
            _const_spec(out_gain.shape), _const_spec(w_out.shape),
        ],
        out_specs=_tile_spec(tile, d),
        out_shape=jax.ShapeDtypeStruct(x.shape, x.dtype),
        scratch_shapes=[
            pltpu.VMEM((REC_HEADS, REC_HEAD_DIM, REC_HEAD_DIM), F32),
            pltpu.VMEM((POOL_HALO, POOL_WIDTH), F32),
        ],
        compiler_params=_params(("arbitrary", "arbitrary")),
        name="even_mixer",
    )(x, g0, g1, w_in, pool_w, pool_scale, lb, out_gain, w_out)


def _odd_kernel(x_ref, g0_ref, g1_ref, w_qk_ref, w_vt_ref, bias_ref, w_ot_ref, o_ref, k_ref, vt_ref,
                s_ref):
    tile = x_ref.shape[1]
    d = x_ref.shape[2]
    keys = BAND_ROWS + tile
    n_slab = keys // tile
    j = pl.program_id(1)

    def slab(b):
        return lax.rem(j + b, n_slab)

    @pl.when(j == 0)
    def _():
        k_ref[:n_slab - 1] = jnp.zeros((n_slab - 1, tile, d), BF16)
        vt_ref[:n_slab - 1] = jnp.zeros((n_slab - 1, d, tile), BF16)

    x = x_ref[0]
    h = _rmsnorm(x, g0_ref[...]).astype(BF16)
    qk = _dot(h, w_qk_ref[...])
    q = (qk[:, :d] * (ATT_HEAD_DIM ** -0.5)).astype(BF16)
    k_ref[slab(n_slab - 1)] = qk[:, d:].astype(BF16)
    vt_ref[slab(n_slab - 1)] = _dot_nt(w_vt_ref[...], h).astype(BF16)

    low_half = lax.broadcasted_iota(jnp.int32, (1, LANES), 1) < ATT_HEAD_DIM

    def lanes(head):
        return slice(head // 2 * LANES, (head // 2 + 1) * LANES)

    def score_stage(head):
        own = low_half if head % 2 == 0 else jnp.logical_not(low_half)
        qp = q[:, lanes(head)]
        qm = jnp.where(own, qp, jnp.zeros_like(qp))
        kp = jnp.concatenate([k_ref[slab(b), :, lanes(head)] for b in range(n_slab)], axis=0)
        s = _dot_nt(kp, qm) + bias_ref[head]
        for r in range(0, keys, tile):
            blk = s[r:r + tile, :]
            if r < BAND_ROWS:
                blk = blk + jnp.where(j * tile - BAND_ROWS + r >= 0, 0.0, NEG_BIG)
            s_ref[head % SCORE_SLOTS, r:r + tile, :] = blk

    def value_stage(head):
        s = s_ref[head % SCORE_SLOTS]
        m = jnp.max(s, axis=0, keepdims=True)
        pr = jnp.exp(s - m)
        l = jnp.sum(pr, axis=0, keepdims=True)
        vt = jnp.concatenate([vt_ref[slab(b), lanes(head), :] for b in range(n_slab)], axis=1)
        return _dot(vt, pr.astype(BF16)) / l

    outs = []
    ahead = SCORE_SLOTS - 1
    for head in range(ahead):
        score_stage(head)
    for head in range(ATT_HEADS):
        if head + ahead < ATT_HEADS:
            score_stage(head + ahead)
        o = value_stage(head)
        outs.append(o[:ATT_HEAD_DIM] if head % 2 == 0 else o[ATT_HEAD_DIM:])
    att_t = jnp.concatenate(outs, axis=0).astype(BF16)
    yt = _dot(w_ot_ref[...], att_t)
    yt = yt * lax.rsqrt(jnp.mean(yt * yt, axis=0, keepdims=True) + NORM_EPS)
    o_ref[0] = x + yt.T * g1_ref[...]


def _band_bias(rel_bias, tile):
    heads = rel_bias.shape[0]
    keys = BAND_ROWS + tile
    n = tile + keys - 1
    offs = jnp.arange(n) - (tile - 1)
    idx = jnp.clip(BAND_ROWS - offs, -REL_CLIP, REL_CLIP) + REL_CLIP
    per_offset = jnp.pad(rel_bias[:, idx].astype(F32), ((0, 0), (0, 1)))
    flat = jnp.tile(per_offset, (1, tile + 1))[:, tile - 1:tile - 1 + tile * n]
    toeplitz = flat.reshape(heads, tile, n)[:, :, :keys]
    r = jnp.arange(tile)[:, None]
    c = jnp.arange(keys)[None, :]
    lag = (r // CHUNK + BAND_CHUNKS) - c // CHUNK
    in_band = (lag >= 0) & (lag <= BAND_CHUNKS)
    return jnp.swapaxes(jnp.where(in_band[None], toeplitz, NEG_BIG), 1, 2)


def _odd_mixer(x, g0, g1, w_qk, w_vt, bias, w_ot):
    b, s, d = x.shape
    tile = MIX_TILE
    return pl.pallas_call(
        _odd_kernel,
        grid=(b, s // tile),
        in_specs=[
            _tile_spec(tile, d),
            _const_spec(g0.shape), _const_spec(g1.shape), _const_spec(w_qk.shape),
            _const_spec(w_vt.shape), _const_spec(bias.shape), _const_spec(w_ot.shape),
        ],
        out_specs=_tile_spec(tile, d),
        out_shape=jax.ShapeDtypeStruct(x.shape, x.dtype),
        scratch_shapes=[
            pltpu.VMEM(((BAND_ROWS + tile) // tile, tile, d), BF16),
            pltpu.VMEM(((BAND_ROWS + tile) // tile, d, tile), BF16),
            pltpu.VMEM((SCORE_SLOTS, BAND_ROWS + tile, tile), F32),
        ],
        compiler_params=_params(("arbitrary", "arbitrary")),
        name="odd_mixer",
    )(x, g0, g1, w_qk, w_vt, bias, w_ot)


def _mem_kv_kernel(mem_ref, g_ref, w_kv_ref, kv_ref):
    mem_n = _rmsnorm(mem_ref[0], g_ref[...]).astype(BF16)
    kv_ref[0, 0] = _dot(mem_n, w_kv_ref[0]).astype(BF16)


def _mem_kv(mem, gain, w_kv):
    b, m, d = mem.shape
    depth = w_kv.shape[0]
    return pl.pallas_call(
        _mem_kv_kernel,
        grid=(depth, b),
        in_specs=[
            pl.BlockSpec((1, m, d), lambda l, i: (i, 0, 0)),
            pl.BlockSpec((1, d), lambda l, i: (0, 0)),
            pl.BlockSpec((1, d, 2 * d), lambda l, i: (l, 0, 0)),
        ],
        out_specs=pl.BlockSpec((1, 1, m, 2 * d), lambda l, i: (l, i, 0, 0)),
        out_shape=jax.ShapeDtypeStruct((depth, b, m, 2 * d), BF16),
        compiler_params=_params(("arbitrary", "arbitrary")),
        name="mem_kv",
    )(mem, gain, w_kv)


def _mem_attn_kernel(x_ref, g2_ref, g3_ref, w_q_ref, kv_ref, w_o_ref, o_ref):
    d = x_ref.shape[2]
    dh = d // MEM_HEADS
    x = x_ref[0]
    h = _rmsnorm(x, g2_ref[...]).astype(BF16)
    q = (_dot(h, w_q_ref[...]) * (dh ** -0.5)).astype(BF16)
    outs = []
    for hh in range(MEM_HEADS):
        sl = slice(hh * dh, (hh + 1) * dh)
        kh = kv_ref[0, :, sl]
        vh = kv_ref[0, :, d + hh * dh:d + (hh + 1) * dh]
        s = _dot_nt(q[:, sl], kh)
        m = jnp.max(s, axis=-1, keepdims=True)
        pr = jnp.exp(s - m)
        l = jnp.sum(pr, axis=-1, keepdims=True)
        outs.append(_dot(pr.astype(BF16), vh) / l)
    att = jnp.concatenate(outs, axis=-1).astype(BF16)
    y = _dot(att, w_o_ref[...])
    o_ref[0] = x + _rmsnorm(y, g3_ref[...])


def _mem_attn(x, g2, g3, w_q, kv, w_o, tile):
    b, s, d = x.shape
    m = kv.shape[1]
    return pl.pallas_call(
        _mem_attn_kernel,
        grid=(b, s // tile),
        in_specs=[
            _tile_spec(tile, d),
            _const_spec(g2.shape), _const_spec(g3.shape), _const_spec(w_q.shape),
            pl.BlockSpec((1, m, 2 * d), lambda i, j: (i, 0, 0)),
            _const_spec(w_o.shape),
        ],
        out_specs=_tile_spec(tile, d),
        out_shape=jax.ShapeDtypeStruct(x.shape, x.dtype),
        compiler_params=_params(("arbitrary", "arbitrary")),
        name="mem_attn",
    )(x, g2, g3, w_q, kv, w_o)


def _mlp_kernel(x_ref, g4_ref, g5_ref, w1_ref, w2_ref, o_ref):
    x = x_ref[0]
    h = _rmsnorm(x, g4_ref[...]).astype(BF16)
    a = jnp.maximum(_dot(h, w1_ref[...]), 0.0)
    y = _dot((a * a).astype(BF16), w2_ref[...])
    o_ref[0] = x + _rmsnorm(y, g5_ref[...])


def _mlp(x, g4, g5, w1, w2, tile):
    b, s, d = x.shape
    return pl.pallas_call(
        _mlp_kernel,
        grid=(b, s // tile),
        in_specs=[
            _tile_spec(tile, d),
            _const_spec(g4.shape), _const_spec(g5.shape), _const_spec(w1.shape),
            _const_spec(w2.shape),
        ],
        out_specs=_tile_spec(tile, d),
        out_shape=jax.ShapeDtypeStruct(x.shape, x.dtype),
        compiler_params=_params(("arbitrary", "arbitrary")),
        name="mlp",
    )(x, g4, g5, w1, w2)


def kernel(x, mem, norm_gains, mem_norm_gain, w_in_ab, pool_w, pool_scale, rec_lb_logits,
           rec_out_gain, w_out_ab, w_qkv, rel_bias, w_o_att, w_mem_q, w_mem_kv, w_mem_o,
           w_ff1, w_ff2):
    depth = norm_gains.shape[0]
    s = x.shape[1]
    dense_tile = min(512, s)
    row = lambda a: a.reshape(1, -1)
    lb_p = jax.nn.softmax(rec_lb_logits.astype(F32), axis=0)
    lbs = jnp.clip(jnp.cumsum(lb_p, axis=0) - lb_p[0], 0.0, 1.0)
    kv = _mem_kv(mem, row(mem_norm_gain), w_mem_kv.astype(BF16))
    for l in range(depth):
        g = norm_gains[l]
        if l % 2 == 0:
            e = l // 2
            x = _even_mixer(x, row(g[0]), row(g[1]), w_in_ab[e].astype(BF16),
                            pool_w[e].astype(BF16), row(pool_scale[e]), row(lbs[e]),
                            row(rec_out_gain[e]), w_out_ab[e].astype(BF16))
        else:
            o = l // 2
            d = x.shape[2]
            x = _odd_mixer(x, row(g[0]), row(g[1]), w_qkv[o, :, :2 * d].astype(BF16),
                           w_qkv[o, :, 2 * d:].T.astype(BF16), _band_bias(rel_bias[o], MIX_TILE),
                           w_o_att[o].T.astype(BF16))
        x = _mem_attn(x, row(g[2]), row(g[3]), w_mem_q[l].astype(BF16), kv[l],
                      w_mem_o[l].astype(BF16), dense_tile)
        x = _mlp(x, row(g[4]), row(g[5]), w_ff1[l].astype(BF16), w_ff2[l].astype(BF16),
                 dense_tile)
    return x
```

```python
import functools

import jax
import jax.numpy as jnp
from jax import lax
from jax.experimental import pallas as pl
from jax.experimental.pallas import tpu as pltpu

NORM_EPS = 1e-6
NEG_BIG = -1e30
GATE_CLIP = 60.0
CHUNK = 64
POOL_WINDOWS = (2, 4, 8, 16)
POOL_GROUP_DIM = 128
POOL_WIDTH = 512
REC_WIDTH = 512
REC_HEADS = 4
REC_HEAD_DIM = 128
ATT_HEADS = 16
ATT_HEAD_DIM = 64
BAND_CHUNKS = 8
REL_CLIP = 256
MEM_HEADS = 4

LANES = 128
SUBLANES = 8
MIX_TILE = 256
DENSE_TILE = 1024
DENSE_GROUPS = 4
POOL_HALO = 16
BAND_ROWS = BAND_CHUNKS * CHUNK
SCORE_SLOTS = 6
CAST_BLOCK_ELEMS = 1 << 20
BF16_ROWS = 16
LOG2_E = 1.4426950408889634
VMEM_LIMIT = 56 * 1024 * 1024

BF16 = jnp.bfloat16
F32 = jnp.float32


def _rmsnorm(x, g):
    return x * lax.rsqrt(jnp.mean(x * x, axis=-1, keepdims=True) + NORM_EPS) * g


def _dot(a, b):
    return jnp.dot(a, b, preferred_element_type=F32)


def _dot_nt(a, b):
    return lax.dot_general(a, b, (((1,), (1,)), ((), ())), preferred_element_type=F32)


def _dot_tn(a, b):
    return lax.dot_general(a, b, (((0,), (0,)), ((), ())), preferred_element_type=F32)


def _const_spec(shape):
    nd = len(shape)
    return pl.BlockSpec(shape, lambda *_: (0,) * nd, pipeline_mode=pl.Buffered(1))


def _layer_spec(stack, layer):
    nd = stack.ndim
    return pl.BlockSpec((1,) + stack.shape[1:], lambda *_: (layer,) + (0,) * (nd - 1),
                        pipeline_mode=pl.Buffered(1))


def _tile_spec(tile, d):
    return pl.BlockSpec((1, tile, d), lambda b, j: (b, j, 0))


def _params(sem):
    return pltpu.CompilerParams(dimension_semantics=sem, vmem_limit_bytes=VMEM_LIMIT)


def _cast_kernel(x_ref, o_ref):
    o_ref[...] = x_ref[...].astype(o_ref.dtype)


def _to_bf16(w):
    cols = w.shape[-1]
    flat = w.reshape(-1, cols)
    rows = flat.shape[0]
    block = min(rows, 1 << ((CAST_BLOCK_ELEMS // cols).bit_length() - 1))
    assert rows % block == 0 and cols % LANES == 0
    spec = pl.BlockSpec((block, cols), lambda i: (i, 0))
    out = pl.pallas_call(
        _cast_kernel,
        grid=(rows // block,),
        in_specs=[spec],
        out_specs=spec,
        out_shape=jax.ShapeDtypeStruct(flat.shape, BF16),
        compiler_params=_params(("arbitrary",)),
        name="to_bf16",
    )(flat)
    return out.reshape(w.shape)


def _even_kernel(x_ref, g0_ref, g1_ref, w_in_ref, pool_w_ref, pool_scale_ref, lb_ref,
                 out_gain_ref, w_out_ref, o_ref, state_ref, tail_ref):
    tile = x_ref.shape[1]
    j = pl.program_id(1)

    @pl.when(j == 0)
    def _():
        state_ref[...] = jnp.zeros_like(state_ref)
        tail_ref[...] = jnp.zeros_like(tail_ref)

    x = x_ref[0]
    h = _rmsnorm(x, g0_ref[...]).astype(BF16)
    z = _dot(h, w_in_ref[0])
    u = z[:, :POOL_WIDTH]

    cat = jnp.concatenate([tail_ref[...], u], axis=0)
    tail_ref[...] = u[tile - POOL_HALO:, :]
    row = lax.broadcasted_iota(jnp.int32, (tile, 1), 0)
    pos = (j * tile + row + 1).astype(F32)
    outs = []
    for g, w in enumerate(POOL_WINDOWS):
        sl = slice(g * POOL_GROUP_DIM, (g + 1) * POOL_GROUP_DIM)
        s = cat[:, sl]
        for k in range(g + 1):
            s = s + pltpu.roll(s, 1 << k, axis=0)
        mean = s[POOL_HALO:, :] / jnp.minimum(pos, float(w))
        pooled = (mean - u[:, sl]).astype(BF16)
        outs.append(_dot(pooled, pool_w_ref[g]))
    a_out = jnp.concatenate(outs, axis=-1) * pool_scale_ref[...]

    rows = lax.broadcasted_iota(jnp.int32, (tile, LANES), 0)
    ri = lax.broadcasted_iota(jnp.int32, (SUBLANES, LANES), 0)
    ci = lax.broadcasted_iota(jnp.int32, (SUBLANES, LANES), 1)
    n_row, n_lane = tile // SUBLANES, tile // LANES
    split = [(ri + SUBLANES * o) ^ ci for o in range(LANES // SUBLANES)]
    cols = [ci + LANES * l for l in range(n_lane)]
    b_outs = []
    for hh in range(REC_HEADS):
        def col(r, hh=hh):
            base = POOL_WIDTH + r * REC_WIDTH + hh * REC_HEAD_DIM
            return z[:, base:base + REC_HEAD_DIM]
        q, fz, v, gate = col(0), col(1), col(2), col(3)
        lb = lb_ref[:, hh * REC_HEAD_DIM:(hh + 1) * REC_HEAD_DIM]
        a = jnp.exp(-jnp.clip(fz, -GATE_CLIP, GATE_CLIP))
        inv = 1.0 / (1.0 + a)
        logf = jnp.log2((1.0 + lb * a) * inv)
        kk = (1.0 - lb) * (a * inv)
        vb = v.astype(BF16)

        def piece(p, r, l):
            return p[r * SUBLANES:(r + 1) * SUBLANES, l * LANES:(l + 1) * LANES]

        def diag_lane(r):
            return r * SUBLANES // LANES

        def diag_split(r):
            return split[r % (LANES // SUBLANES)]

        p = _dot_nt(q.astype(BF16), kk.astype(BF16))
        sc = [[jnp.where(diag_split(r) == 0, piece(p, r, l), 0.0) if l == diag_lane(r)
               else jnp.zeros((SUBLANES, LANES), F32) for l in range(n_lane)] for r in range(n_row)]

        c = logf
        tot = logf
        half = 1
        while half < SUBLANES:
            upper = (rows & half) != 0
            e = jnp.exp2(jnp.where(upper, c, tot - c))
            qs = jnp.where(upper, q * e, 0.0).astype(BF16)
            ks = jnp.where(upper, 0.0, kk * e).astype(BF16)
            p = _dot_nt(qs, ks)
            for r in range(n_row):
                l = diag_lane(r)
                level = (diag_split(r) >> (half.bit_length() - 1)) == 1
                sc[r][l] = jnp.where(level, piece(p, r, l), sc[r][l])
            dn = pltpu.roll(tot, half, axis=0)
            up = pltpu.roll(tot, tile - half, axis=0)
            c = c + jnp.where(upper, dn, 0.0)
            tot = tot + jnp.where(upper, dn, up)
            half *= 2

        cb = [c[i * half:(i + 1) * half] for i in range(tile // half)]
        tb = [tot[i * half:i * half + 1] for i in range(tile // half)]
        while half < tile:
            n_blk = tile // half
            qs, ks = [], []
            for i in range(0, n_blk, 2):
                lo = slice(i * half, (i + 1) * half)
                hi = slice((i + 1) * half, (i + 2) * half)
                ks.append((kk[lo] * jnp.exp2(tb[i] - cb[i])).astype(BF16))
                ks.append(jnp.zeros((half, LANES), BF16))
                qs.append((q[hi] * jnp.exp2(cb[i + 1])).astype(BF16))
            p = _dot_nt(jnp.concatenate(qs, axis=0), jnp.concatenate(ks, axis=0))
            for pair in range(n_blk // 2):
                col0 = 2 * pair * half
                for a in range(half // SUBLANES):
                    r = (2 * pair + 1) * half // SUBLANES + a
                    for l in range(col0 // LANES, (col0 + half - 1) // LANES + 1):
                        new = piece(p, pair * half // SUBLANES + a, l)
                        if 2 * half < LANES:
                            same_pair = (cols[l] >> (2 * half).bit_length() - 1) == pair
                            new = jnp.where(same_pair, new, 0.0)
                        sc[r][l] = sc[r][l] + new
            cb = [jnp.concatenate([cb[i], cb[i + 1] + tb[i]], axis=0) for i in range(0, n_blk, 2)]
            tb = [tb[i] + tb[i + 1] for i in range(0, n_blk, 2)]
            half *= 2
        c, tot = cb[0], tb[0]
        scores = jnp.concatenate([jnp.concatenate(r, axis=1) for r in sc], axis=0)

        st = state_ref[hh]
        o = _dot(scores.astype(BF16), vb) + _dot_nt((q * jnp.exp2(c)).astype(BF16), st.astype(BF16))
        k_end = (kk * jnp.exp2(tot - c)).astype(BF16)
        state_ref[hh] = st * jnp.exp2(tot) + _dot_tn(vb, k_end)
        o = _rmsnorm(o, out_gain_ref[...])
        b_outs.append(o * (gate * jax.nn.sigmoid(gate)))
    mixed = jnp.concatenate([a_out] + b_outs, axis=-1).astype(BF16)
    y = _dot(mixed, w_out_ref[0])
    o_ref[0] = x + _rmsnorm(y, g1_ref[...])


def _even_mixer(x, g0, g1, w_in, pool_w, pool_scale, lb, out_gain, w_out, layer):
    b, s, d = x.shape
    tile = MIX_TILE
    return pl.pallas_call(
        _even_kernel,
        grid=(b, s // tile),
        in_specs=[
            _tile_spec(tile, d),
            _const_spec(g0.shape), _const_spec(g1.shape), _layer_spec(w_in, layer),
            _const_spec(pool_w.shape), _const_spec(pool_scale.shape), _const_spec(lb.shape),
            _const_spec(out_gain.shape), _layer_spec(w_out, layer),
        ],
        out_specs=_tile_spec(tile, d),
        out_shape=jax.ShapeDtypeStruct(x.shape, x.dtype),
        scratch_shapes=[
            pltpu.VMEM((REC_HEADS, REC_HEAD_DIM, REC_HEAD_DIM), F32),
            pltpu.VMEM((POOL_HALO, POOL_WIDTH), F32),
        ],
        compiler_params=_params(("arbitrary", "arbitrary")),
        name="even_mixer",
    )(x, g0, g1, w_in, pool_w, pool_scale, lb, out_gain, w_out)


def _odd_kernel(x_ref, g0_ref, g1_ref, w_qk_ref, w_vt_ref, bias_ref, w_ot_ref, o_ref, k_ref, vt_ref,
                s_ref):
    tile = x_ref.shape[1]
    d = x_ref.shape[2]
    keys = BAND_ROWS + tile
    n_slab = keys // tile
    j = pl.program_id(1)

    def slab(b):
        return lax.rem(j + b, n_slab)

    @pl.when(j == 0)
    def _():
        k_ref[:n_slab - 1] = jnp.zeros((n_slab - 1, tile, d), BF16)
        vt_ref[:n_slab - 1] = jnp.zeros((n_slab - 1, d, tile), BF16)

    x = x_ref[0]
    h = _rmsnorm(x, g0_ref[...]).astype(BF16)
    qk = _dot(h, w_qk_ref[...])
    q = (qk[:, :d] * (ATT_HEAD_DIM ** -0.5 * LOG2_E)).astype(BF16)
    k_ref[slab(n_slab - 1)] = qk[:, d:].astype(BF16)
    vt_ref[slab(n_slab - 1)] = _dot_nt(w_vt_ref[...], h).astype(BF16)

    low_half = lax.broadcasted_iota(jnp.int32, (1, LANES), 1) < ATT_HEAD_DIM
    ones_rows = jnp.ones((BF16_ROWS, keys), BF16)

    def lanes(head):
        return slice(head // 2 * LANES, (head // 2 + 1) * LANES)

    def score_stage(head):
        own = low_half if head % 2 == 0 else jnp.logical_not(low_half)
        qp = q[:, lanes(head)]
        qm = jnp.where(own, qp, jnp.zeros_like(qp))
        kp = jnp.concatenate([k_ref[slab(b), :, lanes(head)] for b in range(n_slab)], axis=0)
        s = _dot_nt(kp, qm) + bias_ref[head]
        for r in range(0, keys, tile):
            blk = s[r:r + tile, :]
            if r < BAND_ROWS:
                blk = blk + jnp.where(j * tile - BAND_ROWS + r >= 0, 0.0, NEG_BIG)
            s_ref[head % SCORE_SLOTS, r:r + tile, :] = blk

    def value_stage(head):
        s = s_ref[head % SCORE_SLOTS]
        m = jnp.max(s, axis=0, keepdims=True)
        pr = jnp.exp2(s - m).astype(BF16)
        vt = jnp.concatenate([vt_ref[slab(b), lanes(head), :] for b in range(n_slab)], axis=1)
        o = _dot(jnp.concatenate([vt, ones_rows], axis=0), pr)
        return o[:LANES] / o[LANES:LANES + 1]

    outs = []
    ahead = SCORE_SLOTS - 1
    for head in range(ahead):
        score_stage(head)
    for head in range(ATT_HEADS):
        if head + ahead < ATT_HEADS:
            score_stage(head + ahead)
        o = value_stage(head)
        outs.append(o[:ATT_HEAD_DIM] if head % 2 == 0 else o[ATT_HEAD_DIM:])
    att_t = jnp.concatenate(outs, axis=0).astype(BF16)
    yt = _dot(w_ot_ref[...], att_t)
    yt = yt * lax.rsqrt(jnp.mean(yt * yt, axis=0, keepdims=True) + NORM_EPS)
    o_ref[0] = x + yt.T * g1_ref[...]


def _band_bias(rel_bias, tile):
    heads = rel_bias.shape[0]
    band = BAND_ROWS + CHUNK
    n = CHUNK + band - 1
    offs = jnp.arange(n) - (CHUNK - 1)
    idx = jnp.clip(BAND_ROWS - offs, -REL_CLIP, REL_CLIP) + REL_CLIP
    per_offset = jnp.pad(rel_bias[:, idx].astype(F32), ((0, 0), (0, 1)))
    flat = jnp.tile(per_offset, (1, CHUNK + 1))[:, CHUNK - 1:CHUNK - 1 + CHUNK * n]
    block = jnp.swapaxes(flat.reshape(heads, CHUNK, n)[:, :, :band], 1, 2)
    n_chunk = tile // CHUNK
    cols = [jnp.pad(block, ((0, 0), (qc * CHUNK, (n_chunk - 1 - qc) * CHUNK), (0, 0)),
                    constant_values=NEG_BIG) for qc in range(n_chunk)]
    return jnp.concatenate(cols, axis=2)


def _odd_mixer(x, g0, g1, w_qk, w_vt, bias, w_ot):
    b, s, d = x.shape
    tile = MIX_TILE
    return pl.pallas_call(
        _odd_kernel,
        grid=(b, s // tile),
        in_specs=[
            _tile_spec(tile, d),
            _const_spec(g0.shape), _const_spec(g1.shape), _const_spec(w_qk.shape),
            _const_spec(w_vt.shape), _const_spec(bias.shape), _const_spec(w_ot.shape),
        ],
        out_specs=_tile_spec(tile, d),
        out_shape=jax.ShapeDtypeStruct(x.shape, x.dtype),
        scratch_shapes=[
            pltpu.VMEM(((BAND_ROWS + tile) // tile, tile, d), BF16),
            pltpu.VMEM(((BAND_ROWS + tile) // tile, d, tile), BF16),
            pltpu.VMEM((SCORE_SLOTS, BAND_ROWS + tile, tile), F32),
        ],
        compiler_params=_params(("arbitrary", "arbitrary")),
        name="odd_mixer",
    )(x, g0, g1, w_qk, w_vt, bias, w_ot)


def _mem_kv_kernel(mem_ref, g_ref, w_kv_ref, kv_ref):
    mem_n = _rmsnorm(mem_ref[0], g_ref[...]).astype(BF16)
    kv_ref[0, 0] = _dot(mem_n, w_kv_ref[0]).astype(BF16)


def _mem_kv(mem, gain, w_kv):
    b, m, d = mem.shape
    depth = w_kv.shape[0]
    return pl.pallas_call(
        _mem_kv_kernel,
        grid=(depth, b),
        in_specs=[
            pl.BlockSpec((1, m, d), lambda l, i: (i, 0, 0)),
            pl.BlockSpec((1, d), lambda l, i: (0, 0)),
            pl.BlockSpec((1, d, 2 * d), lambda l, i: (l, 0, 0)),
        ],
        out_specs=pl.BlockSpec((1, 1, m, 2 * d), lambda l, i: (l, i, 0, 0)),
        out_shape=jax.ShapeDtypeStruct((depth, b, m, 2 * d), BF16),
        compiler_params=_params(("arbitrary", "arbitrary")),
        name="mem_kv",
    )(mem, gain, w_kv)


def _mem_attn_kernel(x_ref, g2_ref, g3_ref, w_q_ref, kv_ref, w_o_ref, o_ref, s_ref):
    d = x_ref.shape[2]
    dh = d // MEM_HEADS
    rows = x_ref.shape[1] // DENSE_GROUPS
    xs, atts = {}, {}

    def score_stage(r):
        xs[r] = x_ref[0, r * rows:(r + 1) * rows, :]
        h = _rmsnorm(xs[r], g2_ref[...]).astype(BF16)
        q = (_dot(h, w_q_ref[0]) * (dh ** -0.5 * LOG2_E)).astype(BF16)
        for hh in range(MEM_HEADS):
            sl = slice(hh * dh, (hh + 1) * dh)
            s_ref[r, hh] = _dot_nt(q[:, sl], kv_ref[0, 0, :, sl])

    def value_stage(r):
        outs = []
        for hh in range(MEM_HEADS):
            vh = kv_ref[0, 0, :, d + hh * dh:d + (hh + 1) * dh]
            s = s_ref[r, hh]
            m = jnp.max(s, axis=-1, keepdims=True)
            pr = jnp.exp2(s - m)
            l = jnp.sum(pr, axis=-1, keepdims=True)
            outs.append(_dot(pr.astype(BF16), vh) / l)
        atts[r] = jnp.concatenate(outs, axis=-1).astype(BF16)

    def out_stage(r):
        y = _dot(atts[r], w_o_ref[0])
        o_ref[0, r * rows:(r + 1) * rows, :] = xs[r] + _rmsnorm(y, g3_ref[...])

    stages = (score_stage, value_stage, out_stage)
    for step in range(DENSE_GROUPS + len(stages) - 1):
        for depth, stage in enumerate(stages):
            if 0 <= step - depth < DENSE_GROUPS:
                stage(step - depth)


def _mem_attn(x, g2, g3, w_q, kv, w_o, layer, tile):
    b, s, d = x.shape
    m = kv.shape[2]
    return pl.pallas_call(
        _mem_attn_kernel,
        grid=(b, s // tile),
        in_specs=[
            _tile_spec(tile, d),
            _const_spec(g2.shape), _const_spec(g3.shape), _layer_spec(w_q, layer),
            pl.BlockSpec((1, 1, m, 2 * d), lambda i, j: (layer, i, 0, 0)),
            _layer_spec(w_o, layer),
        ],
        out_specs=_tile_spec(tile, d),
        out_shape=jax.ShapeDtypeStruct(x.shape, x.dtype),
        scratch_shapes=[pltpu.VMEM((DENSE_GROUPS, MEM_HEADS, tile // DENSE_GROUPS, m), F32)],
        compiler_params=_params(("arbitrary", "arbitrary")),
        name="mem_attn",
    )(x, g2, g3, w_q, kv, w_o)


def _mlp_kernel(x_ref, g4_ref, g5_ref, w1_ref, w2_ref, o_ref):
    rows = x_ref.shape[1] // DENSE_GROUPS
    xs, hs, acts = {}, {}, {}

    def norm_stage(r):
        xs[r] = x_ref[0, r * rows:(r + 1) * rows, :]
        hs[r] = _rmsnorm(xs[r], g4_ref[...]).astype(BF16)

    def up_stage(r):
        a = jnp.maximum(_dot(hs[r], w1_ref[0]), 0.0)
        acts[r] = (a * a).astype(BF16)

    def down_stage(r):
        y = _dot(acts[r], w2_ref[0])
        o_ref[0, r * rows:(r + 1) * rows, :] = xs[r] + _rmsnorm(y, g5_ref[...])

    norm_stage(0)
    for r in range(DENSE_GROUPS):
        up_stage(r)
        if r + 1 < DENSE_GROUPS:
            norm_stage(r + 1)
        if r > 0:
            down_stage(r - 1)
    down_stage(DENSE_GROUPS - 1)


def _mlp(x, g4, g5, w1, w2, layer, tile):
    b, s, d = x.shape
    return pl.pallas_call(
        _mlp_kernel,
        grid=(b, s // tile),
        in_specs=[
            _tile_spec(tile, d),
            _const_spec(g4.shape), _const_spec(g5.shape), _layer_spec(w1, layer),
            _layer_spec(w2, layer),
        ],
        out_specs=_tile_spec(tile, d),
        out_shape=jax.ShapeDtypeStruct(x.shape, x.dtype),
        compiler_params=_params(("arbitrary", "arbitrary")),
        name="mlp",
    )(x, g4, g5, w1, w2)


def kernel(x, mem, norm_gains, mem_norm_gain, w_in_ab, pool_w, pool_scale, rec_lb_logits,
           rec_out_gain, w_out_ab, w_qkv, rel_bias, w_o_att, w_mem_q, w_mem_kv, w_mem_o,
           w_ff1, w_ff2):
    depth = norm_gains.shape[0]
    s = x.shape[1]
    dense_tile = min(DENSE_TILE, s)
    row = lambda a: a.reshape(1, -1)
    lb_p = jax.nn.softmax(rec_lb_logits.astype(F32), axis=0)
    lbs = jnp.clip(jnp.cumsum(lb_p, axis=0) - lb_p[0], 0.0, 1.0)
    w_in_b, w_out_b = _to_bf16(w_in_ab), _to_bf16(w_out_ab)
    w_mem_q_b, w_mem_o_b = _to_bf16(w_mem_q), _to_bf16(w_mem_o)
    w_ff1_b, w_ff2_b = _to_bf16(w_ff1), _to_bf16(w_ff2)
    kv = _mem_kv(mem, row(mem_norm_gain), _to_bf16(w_mem_kv))
    for l in range(depth):
        g = norm_gains[l]
        if l % 2 == 0:
            e = l // 2
            x = _even_mixer(x, row(g[0]), row(g[1]), w_in_b, pool_w[e].astype(BF16),
                            row(pool_scale[e]), row(lbs[e]), row(rec_out_gain[e]), w_out_b, e)
        else:
            o = l // 2
            d = x.shape[2]
            x = _odd_mixer(x, row(g[0]), row(g[1]), w_qkv[o, :, :2 * d].astype(BF16),
                           w_qkv[o, :, 2 * d:].T.astype(BF16),
                           LOG2_E * _band_bias(rel_bias[o], MIX_TILE), w_o_att[o].T.astype(BF16))
        x = _mem_attn(x, row(g[2]), row(g[3]), w_mem_q_b, kv, w_mem_o_b, l, dense_tile)
        x = _mlp(x, row(g[4]), row(g[5]), w_ff1_b, w_ff2_b, l, dense_tile)
    return x
```

```python
import jax
import jax.numpy as jnp
from jax import lax
from jax.experimental import pallas as pl
from jax.experimental.pallas import tpu as pltpu

NORM_EPS = 1e-6
NEG_BIG = -1e30
GATE_CLIP = 60.0
CHUNK = 64
POOL_WINDOWS = (2, 4, 8, 16)
POOL_GROUP_DIM = 128
POOL_WIDTH = 512
REC_WIDTH = 512
REC_HEADS = 4
REC_HEAD_DIM = 128
ATT_HEADS = 16
ATT_HEAD_DIM = 64
BAND_CHUNKS = 8
REL_CLIP = 256
MEM_HEADS = 4

LANES = 128
SUBLANES = 8
MIX_TILE = 256
DENSE_TILE = 1024
DENSE_GROUPS = 4
POOL_HALO = 16
BAND_ROWS = BAND_CHUNKS * CHUNK
SCORE_SLOTS = 6
CAST_BLOCK_ELEMS = 1 << 20
BF16_ROWS = 16
LOG2_E = 1.4426950408889634
VMEM_LIMIT = 56 * 1024 * 1024

BF16 = jnp.bfloat16
F32 = jnp.float32


def _rmsnorm(x, g):
    return x * lax.rsqrt(jnp.mean(x * x, axis=-1, keepdims=True) + NORM_EPS) * g


def _dot(a, b):
    return jnp.dot(a, b, preferred_element_type=F32)


def _dot_nt(a, b):
    return lax.dot_general(a, b, (((1,), (1,)), ((), ())), preferred_element_type=F32)


def _dot_tn(a, b):
    return lax.dot_general(a, b, (((0,), (0,)), ((), ())), preferred_element_type=F32)


def _const_spec(shape):
    nd = len(shape)
    return pl.BlockSpec(shape, lambda *_: (0,) * nd, pipeline_mode=pl.Buffered(1))


def _layer_spec(stack, layer):
    nd = stack.ndim
    return pl.BlockSpec((1,) + stack.shape[1:], lambda *_: (layer,) + (0,) * (nd - 1),
                        pipeline_mode=pl.Buffered(1))


def _tile_spec(tile, d):
    return pl.BlockSpec((1, tile, d), lambda b, j: (b, j, 0))


def _params(sem):
    return pltpu.CompilerParams(dimension_semantics=sem, vmem_limit_bytes=VMEM_LIMIT)


def _cast_kernel(x_ref, o_ref):
    o_ref[...] = x_ref[...].astype(o_ref.dtype)


def _to_bf16(w):
    cols = w.shape[-1]
    flat = w.reshape(-1, cols)
    rows = flat.shape[0]
    block = min(rows, 1 << ((CAST_BLOCK_ELEMS // cols).bit_length() - 1))
    assert rows % block == 0 and cols % LANES == 0
    spec = pl.BlockSpec((block, cols), lambda i: (i, 0))
    out = pl.pallas_call(
        _cast_kernel,
        grid=(rows // block,),
        in_specs=[spec],
        out_specs=spec,
        out_shape=jax.ShapeDtypeStruct(flat.shape, BF16),
        compiler_params=_params(("arbitrary",)),
        name="to_bf16",
    )(flat)
    return out.reshape(w.shape)


def _even_kernel(x_ref, g0_ref, g1_ref, w_in_ref, pool_w_ref, pool_scale_ref, lb_ref,
                 out_gain_ref, w_out_ref, o_ref, state_ref, tail_ref):
    tile = x_ref.shape[1]
    j = pl.program_id(1)

    @pl.when(j == 0)
    def _():
        state_ref[...] = jnp.zeros_like(state_ref)
        tail_ref[...] = jnp.zeros_like(tail_ref)

    x = x_ref[0]
    h = _rmsnorm(x, g0_ref[...]).astype(BF16)
    z = _dot(h, w_in_ref[0])
    u = z[:, :POOL_WIDTH]

    cat = jnp.concatenate([tail_ref[...], u], axis=0)
    tail_ref[...] = u[tile - POOL_HALO:, :]
    row = lax.broadcasted_iota(jnp.int32, (tile, 1), 0)
    pos = (j * tile + row + 1).astype(F32)
    outs = []
    for g, w in enumerate(POOL_WINDOWS):
        sl = slice(g * POOL_GROUP_DIM, (g + 1) * POOL_GROUP_DIM)
        s = cat[:, sl]
        for k in range(g + 1):
            s = s + pltpu.roll(s, 1 << k, axis=0)
        mean = s[POOL_HALO:, :] / jnp.minimum(pos, float(w))
        pooled = (mean - u[:, sl]).astype(BF16)
        outs.append(_dot(pooled, pool_w_ref[g]))
    a_out = jnp.concatenate(outs, axis=-1) * pool_scale_ref[...]

    rows = lax.broadcasted_iota(jnp.int32, (tile, LANES), 0)
    ri = lax.broadcasted_iota(jnp.int32, (SUBLANES, LANES), 0)
    ci = lax.broadcasted_iota(jnp.int32, (SUBLANES, LANES), 1)
    n_row, n_lane = tile // SUBLANES, tile // LANES
    split = [(ri + SUBLANES * o) ^ ci for o in range(LANES // SUBLANES)]
    cols = [ci + LANES * l for l in range(n_lane)]
    b_outs = []
    for hh in range(REC_HEADS):
        def col(r, hh=hh):
            base = POOL_WIDTH + r * REC_WIDTH + hh * REC_HEAD_DIM
            return z[:, base:base + REC_HEAD_DIM]
        q, fz, v, gate = col(0), col(1), col(2), col(3)
        lb = lb_ref[:, hh * REC_HEAD_DIM:(hh + 1) * REC_HEAD_DIM]
        a = jnp.exp(-jnp.clip(fz, -GATE_CLIP, GATE_CLIP))
        inv = 1.0 / (1.0 + a)
        logf = jnp.log2((1.0 + lb * a) * inv)
        kk = (1.0 - lb) * (a * inv)
        vb = v.astype(BF16)

        def piece(p, r, l):
            return p[r * SUBLANES:(r + 1) * SUBLANES, l * LANES:(l + 1) * LANES]

        def diag_lane(r):
            return r * SUBLANES // LANES

        def diag_split(r):
            return split[r % (LANES // SUBLANES)]

        p = _dot_nt(q.astype(BF16), kk.astype(BF16))
        sc = [[jnp.where(diag_split(r) == 0, piece(p, r, l), 0.0) if l == diag_lane(r)
               else jnp.zeros((SUBLANES, LANES), F32) for l in range(n_lane)] for r in range(n_row)]

        c = logf
        tot = logf
        half = 1
        while half < SUBLANES:
            upper = (rows & half) != 0
            e = jnp.exp2(jnp.where(upper, c, tot - c))
            qs = jnp.where(upper, q * e, 0.0).astype(BF16)
            ks = jnp.where(upper, 0.0, kk * e).astype(BF16)
            p = _dot_nt(qs, ks)
            for r in range(n_row):
                l = diag_lane(r)
                level = (diag_split(r) >> (half.bit_length() - 1)) == 1
                sc[r][l] = jnp.where(level, piece(p, r, l), sc[r][l])
            dn = pltpu.roll(tot, half, axis=0)
            up = pltpu.roll(tot, tile - half, axis=0)
            c = c + jnp.where(upper, dn, 0.0)
            tot = tot + jnp.where(upper, dn, up)
            half *= 2

        cb = [c[i * half:(i + 1) * half] for i in range(tile // half)]
        tb = [tot[i * half:i * half + 1] for i in range(tile // half)]
        while half < tile:
            n_blk = tile // half
            qs, ks = [], []
            for i in range(0, n_blk, 2):
                lo = slice(i * half, (i + 1) * half)
                hi = slice((i + 1) * half, (i + 2) * half)
                ks.append((kk[lo] * jnp.exp2(tb[i] - cb[i])).astype(BF16))
                ks.append(jnp.zeros((half, LANES), BF16))
                qs.append((q[hi] * jnp.exp2(cb[i + 1])).astype(BF16))
            p = _dot_nt(jnp.concatenate(qs, axis=0), jnp.concatenate(ks, axis=0))
            for pair in range(n_blk // 2):
                col0 = 2 * pair * half
                for a in range(half // SUBLANES):
                    r = (2 * pair + 1) * half // SUBLANES + a
                    for l in range(col0 // LANES, (col0 + half - 1) // LANES + 1):
                        new = piece(p, pair * half // SUBLANES + a, l)
                        if 2 * half < LANES:
                            same_pair = (cols[l] >> (2 * half).bit_length() - 1) == pair
                            new = jnp.where(same_pair, new, 0.0)
                        sc[r][l] = sc[r][l] + new
            cb = [jnp.concatenate([cb[i], cb[i + 1] + tb[i]], axis=0) for i in range(0, n_blk, 2)]
            tb = [tb[i] + tb[i + 1] for i in range(0, n_blk, 2)]
            half *= 2
        c, tot = cb[0], tb[0]
        scores = jnp.concatenate([jnp.concatenate(r, axis=1) for r in sc], axis=0)

        st = state_ref[hh]
        o = _dot(scores.astype(BF16), vb) + _dot_nt((q * jnp.exp2(c)).astype(BF16), st.astype(BF16))
        k_end = (kk * jnp.exp2(tot - c)).astype(BF16)
        state_ref[hh] = st * jnp.exp2(tot) + _dot_tn(vb, k_end)
        o = _rmsnorm(o, out_gain_ref[...])
        b_outs.append(o * (gate * jax.nn.sigmoid(gate)))
    mixed = jnp.concatenate([a_out] + b_outs, axis=-1).astype(BF16)
    y = _dot(mixed, w_out_ref[0])
    o_ref[0] = x + _rmsnorm(y, g1_ref[...])


def _even_mixer(x, g0, g1, w_in, pool_w, pool_scale, lb, out_gain, w_out, layer):
    b, s, d = x.shape
    tile = MIX_TILE
    return pl.pallas_call(
        _even_kernel,
        grid=(b, s // tile),
        in_specs=[
            _tile_spec(tile, d),
            _const_spec(g0.shape), _const_spec(g1.shape), _layer_spec(w_in, layer),
            _const_spec(pool_w.shape), _const_spec(pool_scale.shape), _const_spec(lb.shape),
            _const_spec(out_gain.shape), _layer_spec(w_out, layer),
        ],
        out_specs=_tile_spec(tile, d),
        out_shape=jax.ShapeDtypeStruct(x.shape, x.dtype),
        scratch_shapes=[
            pltpu.VMEM((REC_HEADS, REC_HEAD_DIM, REC_HEAD_DIM), F32),
            pltpu.VMEM((POOL_HALO, POOL_WIDTH), F32),
        ],
        compiler_params=_params(("arbitrary", "arbitrary")),
        name="even_mixer",
    )(x, g0, g1, w_in, pool_w, pool_scale, lb, out_gain, w_out)


def _odd_kernel(x_ref, g0_ref, g1_ref, w_qk_ref, w_vt_ref, bias_ref, w_ot_ref, o_ref, k_ref, vt_ref,
                s_ref):
    tile = x_ref.shape[1]
    d = x_ref.shape[2]
    keys = BAND_ROWS + tile
    n_slab = keys // tile
    j = pl.program_id(1)

    def slab(b):
        return lax.rem(j + b, n_slab)

    @pl.when(j == 0)
    def _():
        k_ref[:n_slab - 1] = jnp.zeros((n_slab - 1, tile, d), BF16)
        vt_ref[:n_slab - 1] = jnp.zeros((n_slab - 1, d, tile), BF16)

    x = x_ref[0]
    h = _rmsnorm(x, g0_ref[...]).astype(BF16)
    qk = _dot(h, w_qk_ref[...])
    q = (qk[:, :d] * (ATT_HEAD_DIM ** -0.5 * LOG2_E)).astype(BF16)
    k_ref[slab(n_slab - 1)] = qk[:, d:].astype(BF16)
    vt_ref[slab(n_slab - 1)] = _dot_nt(w_vt_ref[...], h).astype(BF16)

    low_half = lax.broadcasted_iota(jnp.int32, (1, LANES), 1) < ATT_HEAD_DIM
    ones_rows = jnp.ones((BF16_ROWS, keys), BF16)

    def lanes(head):
        return slice(head // 2 * LANES, (head // 2 + 1) * LANES)

    def score_stage(head):
        own = low_half if head % 2 == 0 else jnp.logical_not(low_half)
        qp = q[:, lanes(head)]
        qm = jnp.where(own, qp, jnp.zeros_like(qp))
        kp = jnp.concatenate([k_ref[slab(b), :, lanes(head)] for b in range(n_slab)], axis=0)
        s = _dot_nt(kp, qm) + bias_ref[head]
        for r in range(0, keys, tile):
            blk = s[r:r + tile, :]
            if r < BAND_ROWS:
                blk = blk + jnp.where(j * tile - BAND_ROWS + r >= 0, 0.0, NEG_BIG)
            s_ref[head % SCORE_SLOTS, r:r + tile, :] = blk

    def value_stage(head):
        s = s_ref[head % SCORE_SLOTS]
        m = jnp.max(s, axis=0, keepdims=True)
        pr = jnp.exp2(s - m).astype(BF16)
        vt = jnp.concatenate([vt_ref[slab(b), lanes(head), :] for b in range(n_slab)], axis=1)
        o = _dot(jnp.concatenate([vt, ones_rows], axis=0), pr)
        return o[:LANES] / o[LANES:LANES + 1]

    outs = []
    ahead = SCORE_SLOTS - 1
    for head in range(ahead):
        score_stage(head)
    for head in range(ATT_HEADS):
        if head + ahead < ATT_HEADS:
            score_stage(head + ahead)
        o = value_stage(head)
        outs.append(o[:ATT_HEAD_DIM] if head % 2 == 0 else o[ATT_HEAD_DIM:])
    att_t = jnp.concatenate(outs, axis=0).astype(BF16)
    yt = _dot(w_ot_ref[...], att_t)
    yt = yt * lax.rsqrt(jnp.mean(yt * yt, axis=0, keepdims=True) + NORM_EPS)
    o_ref[0] = x + yt.T * g1_ref[...]


def _band_bias(rel_bias, tile):
    heads = rel_bias.shape[0]
    band = BAND_ROWS + CHUNK
    n = CHUNK + band - 1
    offs = jnp.arange(n) - (CHUNK - 1)
    idx = jnp.clip(BAND_ROWS - offs, -REL_CLIP, REL_CLIP) + REL_CLIP
    per_offset = jnp.pad(rel_bias[:, idx].astype(F32), ((0, 0), (0, 1)))
    flat = jnp.tile(per_offset, (1, CHUNK + 1))[:, CHUNK - 1:CHUNK - 1 + CHUNK * n]
    block = jnp.swapaxes(flat.reshape(heads, CHUNK, n)[:, :, :band], 1, 2)
    n_chunk = tile // CHUNK
    cols = [jnp.pad(block, ((0, 0), (qc * CHUNK, (n_chunk - 1 - qc) * CHUNK), (0, 0)),
                    constant_values=NEG_BIG) for qc in range(n_chunk)]
    return jnp.concatenate(cols, axis=2)


def _odd_mixer(x, g0, g1, w_qk, w_vt, bias, w_ot):
    b, s, d = x.shape
    tile = MIX_TILE
    return pl.pallas_call(
        _odd_kernel,
        grid=(b, s // tile),
        in_specs=[
            _tile_spec(tile, d),
            _const_spec(g0.shape), _const_spec(g1.shape), _const_spec(w_qk.shape),
            _const_spec(w_vt.shape), _const_spec(bias.shape), _const_spec(w_ot.shape),
        ],
        out_specs=_tile_spec(tile, d),
        out_shape=jax.ShapeDtypeStruct(x.shape, x.dtype),
        scratch_shapes=[
            pltpu.VMEM(((BAND_ROWS + tile) // tile, tile, d), BF16),
            pltpu.VMEM(((BAND_ROWS + tile) // tile, d, tile), BF16),
            pltpu.VMEM((SCORE_SLOTS, BAND_ROWS + tile, tile), F32),
        ],
        compiler_params=_params(("arbitrary", "arbitrary")),
        name="odd_mixer",
    )(x, g0, g1, w_qk, w_vt, bias, w_ot)


def _mem_kv_kernel(mem_ref, g_ref, w_kv_ref, kv_ref):
    mem_n = _rmsnorm(mem_ref[0], g_ref[...]).astype(BF16)
    kv_ref[0, 0] = _dot(mem_n, w_kv_ref[0]).astype(BF16)


def _mem_kv(mem, gain, w_kv):
    b, m, d = mem.shape
    depth = w_kv.shape[0]
    return pl.pallas_call(
        _mem_kv_kernel,
        grid=(depth, b),
        in_specs=[
            pl.BlockSpec((1, m, d), lambda l, i: (i, 0, 0)),
            pl.BlockSpec((1, d), lambda l, i: (0, 0)),
            pl.BlockSpec((1, d, 2 * d), lambda l, i: (l, 0, 0)),
        ],
        out_specs=pl.BlockSpec((1, 1, m, 2 * d), lambda l, i: (l, i, 0, 0)),
        out_shape=jax.ShapeDtypeStruct((depth, b, m, 2 * d), BF16),
        compiler_params=_params(("arbitrary", "arbitrary")),
        name="mem_kv",
    )(mem, gain, w_kv)


def _dense_kernel(x_ref, g_ref, w_q_ref, kv_ref, w_o_ref, w1_ref, w2_ref, o_ref, s_ref):
    d = x_ref.shape[2]
    dh = d // MEM_HEADS
    rows = x_ref.shape[1] // DENSE_GROUPS
    xs, atts, hs, acts = {}, {}, {}, {}

    def score_stage(r):
        xs[r] = x_ref[0, r * rows:(r + 1) * rows, :]
        h = _rmsnorm(xs[r], g_ref[0:1, :]).astype(BF16)
        q = (_dot(h, w_q_ref[0]) * (dh ** -0.5 * LOG2_E)).astype(BF16)
        for hh in range(MEM_HEADS):
            sl = slice(hh * dh, (hh + 1) * dh)
            s_ref[r, hh] = _dot_nt(q[:, sl], kv_ref[0, 0, :, sl])

    def value_stage(r):
        outs = []
        for hh in range(MEM_HEADS):
            vh = kv_ref[0, 0, :, d + hh * dh:d + (hh + 1) * dh]
            s = s_ref[r, hh]
            m = jnp.max(s, axis=-1, keepdims=True)
            pr = jnp.exp2(s - m)
            l = jnp.sum(pr, axis=-1, keepdims=True)
            outs.append(_dot(pr.astype(BF16), vh) / l)
        atts[r] = jnp.concatenate(outs, axis=-1).astype(BF16)

    def mid_stage(r):
        y = _dot(atts[r], w_o_ref[0])
        xs[r] = xs[r] + _rmsnorm(y, g_ref[1:2, :])
        hs[r] = _rmsnorm(xs[r], g_ref[2:3, :]).astype(BF16)

    def up_stage(r):
        a = jnp.maximum(_dot(hs[r], w1_ref[0]), 0.0)
        acts[r] = (a * a).astype(BF16)

    def down_stage(r):
        y = _dot(acts[r], w2_ref[0])
        o_ref[0, r * rows:(r + 1) * rows, :] = xs[r] + _rmsnorm(y, g_ref[3:4, :])

    stages = (score_stage, value_stage, mid_stage, up_stage, down_stage)
    for step in range(DENSE_GROUPS + len(stages) - 1):
        for depth, stage in enumerate(stages):
            if 0 <= step - depth < DENSE_GROUPS:
                stage(step - depth)


def _dense(x, gains, w_q, kv, w_o, w1, w2, layer, tile):
    b, s, d = x.shape
    m = kv.shape[2]
    return pl.pallas_call(
        _dense_kernel,
        grid=(b, s // tile),
        in_specs=[
            _tile_spec(tile, d),
            _const_spec(gains.shape), _layer_spec(w_q, layer),
            pl.BlockSpec((1, 1, m, 2 * d), lambda i, j: (layer, i, 0, 0)),
            _layer_spec(w_o, layer), _layer_spec(w1, layer), _layer_spec(w2, layer),
        ],
        out_specs=_tile_spec(tile, d),
        out_shape=jax.ShapeDtypeStruct(x.shape, x.dtype),
        scratch_shapes=[pltpu.VMEM((DENSE_GROUPS, MEM_HEADS, tile // DENSE_GROUPS, m), F32)],
        compiler_params=_params(("arbitrary", "arbitrary")),
        name="dense",
    )(x, gains, w_q, kv, w_o, w1, w2)


def kernel(x, mem, norm_gains, mem_norm_gain, w_in_ab, pool_w, pool_scale, rec_lb_logits,
           rec_out_gain, w_out_ab, w_qkv, rel_bias, w_o_att, w_mem_q, w_mem_kv, w_mem_o,
           w_ff1, w_ff2):
    depth = norm_gains.shape[0]
    s = x.shape[1]
    dense_tile = min(DENSE_TILE, s)
    row = lambda a: a.reshape(1, -1)
    lb_p = jax.nn.softmax(rec_lb_logits.astype(F32), axis=0)
    lbs = jnp.clip(jnp.cumsum(lb_p, axis=0) - lb_p[0], 0.0, 1.0)
    w_in_b, w_out_b = _to_bf16(w_in_ab), _to_bf16(w_out_ab)
    w_qkv_b, w_o_b = _to_bf16(w_qkv), _to_bf16(w_o_att)
    w_mem_q_b, w_mem_o_b = _to_bf16(w_mem_q), _to_bf16(w_mem_o)
    w_ff1_b, w_ff2_b = _to_bf16(w_ff1), _to_bf16(w_ff2)
    kv = _mem_kv(mem, row(mem_norm_gain), _to_bf16(w_mem_kv))
    for l in range(depth):
        g = norm_gains[l]
        if l % 2 == 0:
            e = l // 2
            x = _even_mixer(x, row(g[0]), row(g[1]), w_in_b, pool_w[e].astype(BF16),
                            row(pool_scale[e]), row(lbs[e]), row(rec_out_gain[e]), w_out_b, e)
        else:
            o = l // 2
            d = x.shape[2]
            x = _odd_mixer(x, row(g[0]), row(g[1]), w_qkv_b[o, :, :2 * d], w_qkv_b[o, :, 2 * d:].T,
                           LOG2_E * _band_bias(rel_bias[o], MIX_TILE), w_o_b[o].T)
        x = _dense(x, g[2:6], w_mem_q_b, kv, w_mem_o_b, w_ff1_b, w_ff2_b, l, dense_tile)
    return x
```

```python
import jax
import jax.numpy as jnp
from jax import lax
from jax.experimental import pallas as pl
from jax.experimental.pallas import tpu as pltpu

NORM_EPS = 1e-6
NEG_BIG = -1e30
GATE_CLIP = 60.0
CHUNK = 64
POOL_WINDOWS = (2, 4, 8, 16)
POOL_GROUP_DIM = 128
POOL_WIDTH = 512
REC_WIDTH = 512
REC_HEADS = 4
REC_HEAD_DIM = 128
ATT_HEADS = 16
ATT_HEAD_DIM = 64
BAND_CHUNKS = 8
REL_CLIP = 256
MEM_HEADS = 4

LANES = 128
SUBLANES = 8
MIX_TILE = 256
DENSE_TILE = 1024
DENSE_GROUPS = 4
POOL_HALO = 16
BAND_ROWS = BAND_CHUNKS * CHUNK
SCORE_SLOTS = 6
CAST_BLOCK_ELEMS = 1 << 20
BF16_ROWS = 16
LOG2_E = 1.4426950408889634
VMEM_LIMIT = 56 * 1024 * 1024

BF16 = jnp.bfloat16
F32 = jnp.float32


def _rmsnorm(x, g):
    return x * lax.rsqrt(jnp.mean(x * x, axis=-1, keepdims=True) + NORM_EPS) * g


def _dot(a, b):
    return jnp.dot(a, b, preferred_element_type=F32)


def _dot_nt(a, b):
    return lax.dot_general(a, b, (((1,), (1,)), ((), ())), preferred_element_type=F32)


def _dot_tn(a, b):
    return lax.dot_general(a, b, (((0,), (0,)), ((), ())), preferred_element_type=F32)


def _const_spec(shape):
    nd = len(shape)
    return pl.BlockSpec(shape, lambda *_: (0,) * nd, pipeline_mode=pl.Buffered(1))


def _layer_spec(stack, layer):
    nd = stack.ndim
    return pl.BlockSpec((1,) + stack.shape[1:], lambda *_: (layer,) + (0,) * (nd - 1),
                        pipeline_mode=pl.Buffered(1))


def _tile_spec(tile, d):
    return pl.BlockSpec((1, tile, d), lambda b, j: (b, j, 0))


def _params(sem):
    return pltpu.CompilerParams(dimension_semantics=sem, vmem_limit_bytes=VMEM_LIMIT)


def _cast_kernel(x_ref, o_ref):
    o_ref[...] = x_ref[...].astype(o_ref.dtype)


def _to_bf16(w):
    cols = w.shape[-1]
    flat = w.reshape(-1, cols)
    rows = flat.shape[0]
    block = min(rows, 1 << ((CAST_BLOCK_ELEMS // cols).bit_length() - 1))
    assert rows % block == 0 and cols % LANES == 0
    spec = pl.BlockSpec((block, cols), lambda i: (i, 0))
    out = pl.pallas_call(
        _cast_kernel,
        grid=(rows // block,),
        in_specs=[spec],
        out_specs=spec,
        out_shape=jax.ShapeDtypeStruct(flat.shape, BF16),
        compiler_params=_params(("arbitrary",)),
        name="to_bf16",
    )(flat)
    return out.reshape(w.shape)


def _even_kernel(x_ref, g0_ref, g1_ref, w_in_ref, pool_w_ref, pool_scale_ref, lb_ref,
                 out_gain_ref, w_out_ref, o_ref, state_ref, tail_ref):
    tile = x_ref.shape[1]
    j = pl.program_id(1)

    @pl.when(j == 0)
    def _():
        state_ref[...] = jnp.zeros_like(state_ref)
        tail_ref[...] = jnp.zeros_like(tail_ref)

    x = x_ref[0]
    h = _rmsnorm(x, g0_ref[...]).astype(BF16)
    z = _dot(h, w_in_ref[0])
    u = z[:, :POOL_WIDTH]

    cat = jnp.concatenate([tail_ref[...], u], axis=0)
    tail_ref[...] = u[tile - POOL_HALO:, :]
    row = lax.broadcasted_iota(jnp.int32, (tile, 1), 0)
    pos = (j * tile + row + 1).astype(F32)
    outs = []
    for g, w in enumerate(POOL_WINDOWS):
        sl = slice(g * POOL_GROUP_DIM, (g + 1) * POOL_GROUP_DIM)
        s = cat[:, sl]
        for k in range(g + 1):
            s = s + pltpu.roll(s, 1 << k, axis=0)
        mean = s[POOL_HALO:, :] / jnp.minimum(pos, float(w))
        pooled = (mean - u[:, sl]).astype(BF16)
        outs.append(_dot(pooled, pool_w_ref[g]))
    a_out = jnp.concatenate(outs, axis=-1) * pool_scale_ref[...]

    rows = lax.broadcasted_iota(jnp.int32, (tile, LANES), 0)
    ri = lax.broadcasted_iota(jnp.int32, (SUBLANES, LANES), 0)
    ci = lax.broadcasted_iota(jnp.int32, (SUBLANES, LANES), 1)
    n_row, n_lane = tile // SUBLANES, tile // LANES
    split = [(ri + SUBLANES * o) ^ ci for o in range(LANES // SUBLANES)]
    cols = [ci + LANES * l for l in range(n_lane)]
    b_outs = []
    for hh in range(REC_HEADS):
        def col(r, hh=hh):
            base = POOL_WIDTH + r * REC_WIDTH + hh * REC_HEAD_DIM
            return z[:, base:base + REC_HEAD_DIM]
        q, fz, v, gate = col(0), col(1), col(2), col(3)
        lb = lb_ref[:, hh * REC_HEAD_DIM:(hh + 1) * REC_HEAD_DIM]
        a = jnp.exp(-jnp.clip(fz, -GATE_CLIP, GATE_CLIP))
        inv = 1.0 / (1.0 + a)
        logf = jnp.log2((1.0 + lb * a) * inv)
        kk = (1.0 - lb) * (a * inv)
        vb = v.astype(BF16)

        def piece(p, r, l):
            return p[r * SUBLANES:(r + 1) * SUBLANES, l * LANES:(l + 1) * LANES]

        def diag_lane(r):
            return r * SUBLANES // LANES

        def diag_split(r):
            return split[r % (LANES // SUBLANES)]

        p = _dot_nt(q.astype(BF16), kk.astype(BF16))
        sc = [[jnp.where(diag_split(r) == 0, piece(p, r, l), 0.0) if l == diag_lane(r)
               else jnp.zeros((SUBLANES, LANES), F32) for l in range(n_lane)] for r in range(n_row)]

        c = logf
        tot = logf
        half = 1
        while half < SUBLANES:
            upper = (rows & half) != 0
            e = jnp.exp2(jnp.where(upper, c, tot - c))
            qs = jnp.where(upper, q * e, 0.0).astype(BF16)
            ks = jnp.where(upper, 0.0, kk * e).astype(BF16)
            p = _dot_nt(qs, ks)
            for r in range(n_row):
                l = diag_lane(r)
                level = (diag_split(r) >> (half.bit_length() - 1)) == 1
                sc[r][l] = jnp.where(level, piece(p, r, l), sc[r][l])
            dn = pltpu.roll(tot, half, axis=0)
            up = pltpu.roll(tot, tile - half, axis=0)
            c = c + jnp.where(upper, dn, 0.0)
            tot = tot + jnp.where(upper, dn, up)
            half *= 2

        cb = [c[i * half:(i + 1) * half] for i in range(tile // half)]
        tb = [tot[i * half:i * half + 1] for i in range(tile // half)]
        while half < tile:
            n_blk = tile // half
            qs, ks = [], []
            for i in range(0, n_blk, 2):
                lo = slice(i * half, (i + 1) * half)
                hi = slice((i + 1) * half, (i + 2) * half)
                ks.append((kk[lo] * jnp.exp2(tb[i] - cb[i])).astype(BF16))
                ks.append(jnp.zeros((half, LANES), BF16))
                qs.append((q[hi] * jnp.exp2(cb[i + 1])).astype(BF16))
            p = _dot_nt(jnp.concatenate(qs, axis=0), jnp.concatenate(ks, axis=0))
            for pair in range(n_blk // 2):
                col0 = 2 * pair * half
                for a in range(half // SUBLANES):
                    r = (2 * pair + 1) * half // SUBLANES + a
                    for l in range(col0 // LANES, (col0 + half - 1) // LANES + 1):
                        new = piece(p, pair * half // SUBLANES + a, l)
                        if 2 * half < LANES:
                            same_pair = (cols[l] >> (2 * half).bit_length() - 1) == pair
                            new = jnp.where(same_pair, new, 0.0)
                        sc[r][l] = sc[r][l] + new
            cb = [jnp.concatenate([cb[i], cb[i + 1] + tb[i]], axis=0) for i in range(0, n_blk, 2)]
            tb = [tb[i] + tb[i + 1] for i in range(0, n_blk, 2)]
            half *= 2
        c, tot = cb[0], tb[0]
        scores = jnp.concatenate([jnp.concatenate(r, axis=1) for r in sc], axis=0)

        st = state_ref[hh]
        o = _dot(scores.astype(BF16), vb) + _dot_nt((q * jnp.exp2(c)).astype(BF16), st.astype(BF16))
        k_end = (kk * jnp.exp2(tot - c)).astype(BF16)
        state_ref[hh] = st * jnp.exp2(tot) + _dot_tn(vb, k_end)
        o = _rmsnorm(o, out_gain_ref[...])
        b_outs.append(o * (gate * jax.nn.sigmoid(gate)))
    mixed = jnp.concatenate([a_out] + b_outs, axis=-1).astype(BF16)
    y = _dot(mixed, w_out_ref[0])
    o_ref[0] = x + _rmsnorm(y, g1_ref[...])


def _even_mixer(x, g0, g1, w_in, pool_w, pool_scale, lb, out_gain, w_out, layer):
    b, s, d = x.shape
    tile = MIX_TILE
    return pl.pallas_call(
        _even_kernel,
        grid=(b, s // tile),
        in_specs=[
            _tile_spec(tile, d),
            _const_spec(g0.shape), _const_spec(g1.shape), _layer_spec(w_in, layer),
            _const_spec(pool_w.shape), _const_spec(pool_scale.shape), _const_spec(lb.shape),
            _const_spec(out_gain.shape), _layer_spec(w_out, layer),
        ],
        out_specs=_tile_spec(tile, d),
        out_shape=jax.ShapeDtypeStruct(x.shape, x.dtype),
        scratch_shapes=[
            pltpu.VMEM((REC_HEADS, REC_HEAD_DIM, REC_HEAD_DIM), F32),
            pltpu.VMEM((POOL_HALO, POOL_WIDTH), F32),
        ],
        compiler_params=_params(("arbitrary", "arbitrary")),
        name="even_mixer",
    )(x, g0, g1, w_in, pool_w, pool_scale, lb, out_gain, w_out)


def _odd_kernel(x_ref, g0_ref, g1_ref, w_qk_ref, w_vt_ref, bias_ref, w_ot_ref, o_ref, k_ref, vt_ref,
                s_ref):
    tile = x_ref.shape[1]
    d = x_ref.shape[2]
    keys = BAND_ROWS + tile
    n_slab = keys // tile
    j = pl.program_id(1)

    def slab(b):
        return lax.rem(j + b, n_slab)

    @pl.when(j == 0)
    def _():
        k_ref[:n_slab - 1] = jnp.zeros((n_slab - 1, tile, d), BF16)
        vt_ref[:n_slab - 1] = jnp.zeros((n_slab - 1, d, tile), BF16)

    x = x_ref[0]
    h = _rmsnorm(x, g0_ref[...]).astype(BF16)
    qk = _dot(h, w_qk_ref[...])
    q = (qk[:, :d] * (ATT_HEAD_DIM ** -0.5 * LOG2_E)).astype(BF16)
    k_ref[slab(n_slab - 1)] = qk[:, d:].astype(BF16)
    vt_ref[slab(n_slab - 1)] = _dot_nt(w_vt_ref[...], h).astype(BF16)

    low_half = lax.broadcasted_iota(jnp.int32, (1, LANES), 1) < ATT_HEAD_DIM
    ones_rows = jnp.ones((BF16_ROWS, keys), BF16)

    def lanes(head):
        return slice(head // 2 * LANES, (head // 2 + 1) * LANES)

    def score_stage(head):
        own = low_half if head % 2 == 0 else jnp.logical_not(low_half)
        qp = q[:, lanes(head)]
        qm = jnp.where(own, qp, jnp.zeros_like(qp))
        kp = jnp.concatenate([k_ref[slab(b), :, lanes(head)] for b in range(n_slab)], axis=0)
        s = _dot_nt(kp, qm) + bias_ref[head]
        for r in range(0, keys, tile):
            blk = s[r:r + tile, :]
            if r < BAND_ROWS:
                blk = blk + jnp.where(j * tile - BAND_ROWS + r >= 0, 0.0, NEG_BIG)
            s_ref[head % SCORE_SLOTS, r:r + tile, :] = blk

    def value_stage(head):
        s = s_ref[head % SCORE_SLOTS]
        m = jnp.max(s, axis=0, keepdims=True)
        pr = jnp.exp2(s - m).astype(BF16)
        rows = slice(head * ATT_HEAD_DIM, (head + 1) * ATT_HEAD_DIM)
        vt = jnp.concatenate([vt_ref[slab(b), rows, :] for b in range(n_slab)], axis=1)
        o = _dot(jnp.concatenate([vt, ones_rows], axis=0), pr)
        return o[:ATT_HEAD_DIM] / o[ATT_HEAD_DIM:ATT_HEAD_DIM + 1]

    outs = []
    ahead = SCORE_SLOTS - 1
    for head in range(ahead):
        score_stage(head)
    for head in range(ATT_HEADS):
        if head + ahead < ATT_HEADS:
            score_stage(head + ahead)
        outs.append(value_stage(head))
    att_t = jnp.concatenate(outs, axis=0).astype(BF16)
    yt = _dot(w_ot_ref[...], att_t)
    yt = yt * lax.rsqrt(jnp.mean(yt * yt, axis=0, keepdims=True) + NORM_EPS)
    o_ref[0] = x + yt.T * g1_ref[...]


def _band_bias(rel_bias, tile):
    heads = rel_bias.shape[0]
    band = BAND_ROWS + CHUNK
    n = CHUNK + band - 1
    offs = jnp.arange(n) - (CHUNK - 1)
    idx = jnp.clip(BAND_ROWS - offs, -REL_CLIP, REL_CLIP) + REL_CLIP
    per_offset = jnp.pad(rel_bias[:, idx].astype(F32), ((0, 0), (0, 1)))
    flat = jnp.tile(per_offset, (1, CHUNK + 1))[:, CHUNK - 1:CHUNK - 1 + CHUNK * n]
    block = jnp.swapaxes(flat.reshape(heads, CHUNK, n)[:, :, :band], 1, 2)
    n_chunk = tile // CHUNK
    cols = [jnp.pad(block, ((0, 0), (qc * CHUNK, (n_chunk - 1 - qc) * CHUNK), (0, 0)),
                    constant_values=NEG_BIG) for qc in range(n_chunk)]
    return jnp.concatenate(cols, axis=2)


def _odd_mixer(x, g0, g1, w_qk, w_vt, bias, w_ot):
    b, s, d = x.shape
    tile = MIX_TILE
    return pl.pallas_call(
        _odd_kernel,
        grid=(b, s // tile),
        in_specs=[
            _tile_spec(tile, d),
            _const_spec(g0.shape), _const_spec(g1.shape), _const_spec(w_qk.shape),
            _const_spec(w_vt.shape), _const_spec(bias.shape), _const_spec(w_ot.shape),
        ],
        out_specs=_tile_spec(tile, d),
        out_shape=jax.ShapeDtypeStruct(x.shape, x.dtype),
        scratch_shapes=[
            pltpu.VMEM(((BAND_ROWS + tile) // tile, tile, d), BF16),
            pltpu.VMEM(((BAND_ROWS + tile) // tile, d, tile), BF16),
            pltpu.VMEM((SCORE_SLOTS, BAND_ROWS + tile, tile), F32),
        ],
        compiler_params=_params(("arbitrary", "arbitrary")),
        name="odd_mixer",
    )(x, g0, g1, w_qk, w_vt, bias, w_ot)


def _mem_kv_kernel(mem_ref, g_ref, w_kv_ref, kv_ref):
    b, m, d = mem_ref.shape
    mem_n = _rmsnorm(mem_ref[...].reshape(b * m, d), g_ref[...]).astype(BF16)
    kv_ref[0] = _dot(mem_n, w_kv_ref[0]).astype(BF16).reshape(b, m, 2 * d)


def _mem_kv(mem, gain, w_kv):
    b, m, d = mem.shape
    depth = w_kv.shape[0]
    return pl.pallas_call(
        _mem_kv_kernel,
        grid=(depth,),
        in_specs=[
            pl.BlockSpec((b, m, d), lambda l: (0, 0, 0)),
            pl.BlockSpec((1, d), lambda l: (0, 0)),
            pl.BlockSpec((1, d, 2 * d), lambda l: (l, 0, 0)),
        ],
        out_specs=pl.BlockSpec((1, b, m, 2 * d), lambda l: (l, 0, 0, 0)),
        out_shape=jax.ShapeDtypeStruct((depth, b, m, 2 * d), BF16),
        compiler_params=_params(("arbitrary",)),
        name="mem_kv",
    )(mem, gain, w_kv)


def _dense_kernel(x_ref, g_ref, w_q_ref, kv_ref, w_o_ref, w1_ref, w2_ref, o_ref, s_ref):
    d = x_ref.shape[2]
    dh = d // MEM_HEADS
    rows = x_ref.shape[1] // DENSE_GROUPS
    xs, atts, hs, acts = {}, {}, {}, {}

    def score_stage(r):
        xs[r] = x_ref[0, r * rows:(r + 1) * rows, :]
        h = _rmsnorm(xs[r], g_ref[0:1, :]).astype(BF16)
        q = (_dot(h, w_q_ref[0]) * (dh ** -0.5 * LOG2_E)).astype(BF16)
        for hh in range(MEM_HEADS):
            sl = slice(hh * dh, (hh + 1) * dh)
            s_ref[r, hh] = _dot_nt(q[:, sl], kv_ref[0, 0, :, sl])

    def value_stage(r):
        outs = []
        for hh in range(MEM_HEADS):
            vh = kv_ref[0, 0, :, d + hh * dh:d + (hh + 1) * dh]
            s = s_ref[r, hh]
            m = jnp.max(s, axis=-1, keepdims=True)
            pr = jnp.exp2(s - m)
            l = jnp.sum(pr, axis=-1, keepdims=True)
            outs.append(_dot(pr.astype(BF16), vh) / l)
        atts[r] = jnp.concatenate(outs, axis=-1).astype(BF16)

    def mid_stage(r):
        y = _dot(atts[r], w_o_ref[0])
        xs[r] = xs[r] + _rmsnorm(y, g_ref[1:2, :])
        hs[r] = _rmsnorm(xs[r], g_ref[2:3, :]).astype(BF16)

    def up_stage(r):
        a = jnp.maximum(_dot(hs[r], w1_ref[0]), 0.0)
        acts[r] = (a * a).astype(BF16)

    def down_stage(r):
        y = _dot(acts[r], w2_ref[0])
        o_ref[0, r * rows:(r + 1) * rows, :] = xs[r] + _rmsnorm(y, g_ref[3:4, :])

    stages = (score_stage, value_stage, mid_stage, up_stage, down_stage)
    for step in range(DENSE_GROUPS + len(stages) - 1):
        for depth, stage in enumerate(stages):
            if 0 <= step - depth < DENSE_GROUPS:
                stage(step - depth)


def _dense(x, gains, w_q, kv, w_o, w1, w2, layer, tile):
    b, s, d = x.shape
    m = kv.shape[2]
    return pl.pallas_call(
        _dense_kernel,
        grid=(b, s // tile),
        in_specs=[
            _tile_spec(tile, d),
            _const_spec(gains.shape), _layer_spec(w_q, layer),
            pl.BlockSpec((1, 1, m, 2 * d), lambda i, j: (layer, i, 0, 0)),
            _layer_spec(w_o, layer), _layer_spec(w1, layer), _layer_spec(w2, layer),
        ],
        out_specs=_tile_spec(tile, d),
        out_shape=jax.ShapeDtypeStruct(x.shape, x.dtype),
        scratch_shapes=[pltpu.VMEM((DENSE_GROUPS, MEM_HEADS, tile // DENSE_GROUPS, m), F32)],
        compiler_params=_params(("arbitrary", "arbitrary")),
        name="dense",
    )(x, gains, w_q, kv, w_o, w1, w2)


def kernel(x, mem, norm_gains, mem_norm_gain, w_in_ab, pool_w, pool_scale, rec_lb_logits,
           rec_out_gain, w_out_ab, w_qkv, rel_bias, w_o_att, w_mem_q, w_mem_kv, w_mem_o,
           w_ff1, w_ff2):
    depth = norm_gains.shape[0]
    s = x.shape[1]
    dense_tile = min(DENSE_TILE, s)
    row = lambda a: a.reshape(1, -1)
    lb_p = jax.nn.softmax(rec_lb_logits.astype(F32), axis=0)
    lbs = jnp.clip(jnp.cumsum(lb_p, axis=0) - lb_p[0], 0.0, 1.0)
    w_in_b, w_out_b = _to_bf16(w_in_ab), _to_bf16(w_out_ab)
    w_qkv_b, w_o_b = _to_bf16(w_qkv), _to_bf16(w_o_att)
    w_mem_q_b, w_mem_o_b = _to_bf16(w_mem_q), _to_bf16(w_mem_o)
    w_ff1_b, w_ff2_b = _to_bf16(w_ff1), _to_bf16(w_ff2)
    kv = _mem_kv(mem, row(mem_norm_gain), _to_bf16(w_mem_kv))
    for l in range(depth):
        g = norm_gains[l]
        if l % 2 == 0:
            e = l // 2
            x = _even_mixer(x, row(g[0]), row(g[1]), w_in_b, pool_w[e].astype(BF16),
                            row(pool_scale[e]), row(lbs[e]), row(rec_out_gain[e]), w_out_b, e)
        else:
            o = l // 2
            d = x.shape[2]
            x = _odd_mixer(x, row(g[0]), row(g[1]), w_qkv_b[o, :, :2 * d], w_qkv_b[o, :, 2 * d:].T,
                           LOG2_E * _band_bias(rel_bias[o], MIX_TILE), w_o_b[o].T)
        x = _dense(x, g[2:6], w_mem_q_b, kv, w_mem_o_b, w_ff1_b, w_ff2_b, l, dense_tile)
    return x
```

```python
import jax
import jax.numpy as jnp
from jax import lax
from jax.experimental import pallas as pl
from jax.experimental.pallas import tpu as pltpu

NORM_EPS = 1e-6
NEG_BIG = -1e30
GATE_CLIP = 60.0
CHUNK = 64
POOL_WINDOWS = (2, 4, 8, 16)
POOL_GROUP_DIM = 128
POOL_WIDTH = 512
REC_WIDTH = 512
REC_HEADS = 4
REC_HEAD_DIM = 128
ATT_HEADS = 16
ATT_HEAD_DIM = 64
BAND_CHUNKS = 8
REL_CLIP = 256
MEM_HEADS = 4

LANES = 128
SUBLANES = 8
MIX_TILE = 256
DENSE_TILE = 1024
DENSE_GROUPS = 4
POOL_HALO = 16
BAND_ROWS = BAND_CHUNKS * CHUNK
SCORE_SLOTS = 6
CAST_BLOCK_ELEMS = 1 << 20
BF16_ROWS = 16
LOG2_E = 1.4426950408889634
VMEM_LIMIT = 56 * 1024 * 1024

BF16 = jnp.bfloat16
F32 = jnp.float32


def _rmsnorm(x, g):
    return x * lax.rsqrt(jnp.mean(x * x, axis=-1, keepdims=True) + NORM_EPS) * g


def _dot(a, b):
    return jnp.dot(a, b, preferred_element_type=F32)


def _dot_nt(a, b):
    return lax.dot_general(a, b, (((1,), (1,)), ((), ())), preferred_element_type=F32)


def _dot_tn(a, b):
    return lax.dot_general(a, b, (((0,), (0,)), ((), ())), preferred_element_type=F32)


def _const_spec(shape):
    nd = len(shape)
    return pl.BlockSpec(shape, lambda *_: (0,) * nd, pipeline_mode=pl.Buffered(1))


def _layer_spec(stack, layer):
    nd = stack.ndim
    return pl.BlockSpec((1,) + stack.shape[1:], lambda *_: (layer,) + (0,) * (nd - 1),
                        pipeline_mode=pl.Buffered(1))


def _tile_spec(tile, d):
    return pl.BlockSpec((1, tile, d), lambda b, j: (b, j, 0))


def _params(sem):
    return pltpu.CompilerParams(dimension_semantics=sem, vmem_limit_bytes=VMEM_LIMIT)


def _cast_kernel(x_ref, o_ref):
    o_ref[...] = x_ref[...].astype(o_ref.dtype)


def _to_bf16(w):
    cols = w.shape[-1]
    flat = w.reshape(-1, cols)
    rows = flat.shape[0]
    block = min(rows, 1 << ((CAST_BLOCK_ELEMS // cols).bit_length() - 1))
    assert rows % block == 0 and cols % LANES == 0
    spec = pl.BlockSpec((block, cols), lambda i: (i, 0))
    out = pl.pallas_call(
        _cast_kernel,
        grid=(rows // block,),
        in_specs=[spec],
        out_specs=spec,
        out_shape=jax.ShapeDtypeStruct(flat.shape, BF16),
        compiler_params=_params(("arbitrary",)),
        name="to_bf16",
    )(flat)
    return out.reshape(w.shape)


def _even_kernel(x_ref, g0_ref, g1_ref, w_in_ref, pool_w_ref, pool_scale_ref, lb_ref,
                 out_gain_ref, w_out_ref, o_ref, state_ref, tail_ref):
    tile = x_ref.shape[1]
    j = pl.program_id(1)

    @pl.when(j == 0)
    def _():
        state_ref[...] = jnp.zeros_like(state_ref)
        tail_ref[...] = jnp.zeros_like(tail_ref)

    x = x_ref[0]
    h = _rmsnorm(x, g0_ref[...]).astype(BF16)
    z = _dot(h, w_in_ref[0])
    u = z[:, :POOL_WIDTH]

    cat = jnp.concatenate([tail_ref[...], u], axis=0)
    tail_ref[...] = u[tile - POOL_HALO:, :]
    row = lax.broadcasted_iota(jnp.int32, (tile, 1), 0)
    pos = (j * tile + row + 1).astype(F32)
    outs = []
    for g, w in enumerate(POOL_WINDOWS):
        sl = slice(g * POOL_GROUP_DIM, (g + 1) * POOL_GROUP_DIM)
        s = cat[:, sl]
        for k in range(g + 1):
            s = s + pltpu.roll(s, 1 << k, axis=0)
        mean = s[POOL_HALO:, :] / jnp.minimum(pos, float(w))
        pooled = (mean - u[:, sl]).astype(BF16)
        outs.append(_dot(pooled, pool_w_ref[g]))
    a_out = jnp.concatenate(outs, axis=-1) * pool_scale_ref[...]

    rows = lax.broadcasted_iota(jnp.int32, (tile, LANES), 0)
    ri = lax.broadcasted_iota(jnp.int32, (SUBLANES, LANES), 0)
    ci = lax.broadcasted_iota(jnp.int32, (SUBLANES, LANES), 1)
    n_row, n_lane = tile // SUBLANES, tile // LANES
    split = [(ri + SUBLANES * o) ^ ci for o in range(LANES // SUBLANES)]
    cols = [ci + LANES * l for l in range(n_lane)]
    b_outs = []
    for hh in range(REC_HEADS):
        def col(r, hh=hh):
            base = POOL_WIDTH + r * REC_WIDTH + hh * REC_HEAD_DIM
            return z[:, base:base + REC_HEAD_DIM]
        q, fz, v, gate = col(0), col(1), col(2), col(3)
        lb = lb_ref[:, hh * REC_HEAD_DIM:(hh + 1) * REC_HEAD_DIM]
        a = jnp.exp(-jnp.clip(fz, -GATE_CLIP, GATE_CLIP))
        inv = 1.0 / (1.0 + a)
        logf = jnp.log2((1.0 + lb * a) * inv)
        kk = (1.0 - lb) * (a * inv)
        vb = v.astype(BF16)

        def piece(p, r, l):
            return p[r * SUBLANES:(r + 1) * SUBLANES, l * LANES:(l + 1) * LANES]

        def diag_lane(r):
            return r * SUBLANES // LANES

        def diag_split(r):
            return split[r % (LANES // SUBLANES)]

        p = _dot_nt(q.astype(BF16), kk.astype(BF16))
        sc = [[jnp.where(diag_split(r) == 0, piece(p, r, l), 0.0) if l == diag_lane(r)
               else jnp.zeros((SUBLANES, LANES), F32) for l in range(n_lane)] for r in range(n_row)]

        c = logf
        tot = logf
        half = 1
        while half < SUBLANES:
            upper = (rows & half) != 0
            e = jnp.exp2(jnp.where(upper, c, tot - c))
            qs = jnp.where(upper, q * e, 0.0).astype(BF16)
            ks = jnp.where(upper, 0.0, kk * e).astype(BF16)
            p = _dot_nt(qs, ks)
            for r in range(n_row):
                l = diag_lane(r)
                level = (diag_split(r) >> (half.bit_length() - 1)) == 1
                sc[r][l] = jnp.where(level, piece(p, r, l), sc[r][l])
            dn = pltpu.roll(tot, half, axis=0)
            up = pltpu.roll(tot, tile - half, axis=0)
            c = c + jnp.where(upper, dn, 0.0)
            tot = tot + jnp.where(upper, dn, up)
            half *= 2

        cb = [c[i * half:(i + 1) * half] for i in range(tile // half)]
        tb = [tot[i * half:i * half + 1] for i in range(tile // half)]
        while half < tile:
            n_blk = tile // half
            qs, ks = [], []
            for i in range(0, n_blk, 2):
                lo = slice(i * half, (i + 1) * half)
                hi = slice((i + 1) * half, (i + 2) * half)
                ks.append((kk[lo] * jnp.exp2(tb[i] - cb[i])).astype(BF16))
                ks.append(jnp.zeros((half, LANES), BF16))
                qs.append((q[hi] * jnp.exp2(cb[i + 1])).astype(BF16))
            p = _dot_nt(jnp.concatenate(qs, axis=0), jnp.concatenate(ks, axis=0))
            for pair in range(n_blk // 2):
                col0 = 2 * pair * half
                for a in range(half // SUBLANES):
                    r = (2 * pair + 1) * half // SUBLANES + a
                    for l in range(col0 // LANES, (col0 + half - 1) // LANES + 1):
                        new = piece(p, pair * half // SUBLANES + a, l)
                        if 2 * half < LANES:
                            same_pair = (cols[l] >> (2 * half).bit_length() - 1) == pair
                            new = jnp.where(same_pair, new, 0.0)
                        sc[r][l] = sc[r][l] + new
            cb = [jnp.concatenate([cb[i], cb[i + 1] + tb[i]], axis=0) for i in range(0, n_blk, 2)]
            tb = [tb[i] + tb[i + 1] for i in range(0, n_blk, 2)]
            half *= 2
        c, tot = cb[0], tb[0]
        scores = jnp.concatenate([jnp.concatenate(r, axis=1) for r in sc], axis=0)

        st = state_ref[hh]
        o = _dot(scores.astype(BF16), vb) + _dot_nt((q * jnp.exp2(c)).astype(BF16), st.astype(BF16))
        k_end = (kk * jnp.exp2(tot - c)).astype(BF16)
        state_ref[hh] = st * jnp.exp2(tot) + _dot_tn(vb, k_end)
        o = _rmsnorm(o, out_gain_ref[...])
        b_outs.append(o * (gate * jax.nn.sigmoid(gate)))
    mixed = jnp.concatenate([a_out] + b_outs, axis=-1).astype(BF16)
    y = _dot(mixed, w_out_ref[0])
    o_ref[0] = x + _rmsnorm(y, g1_ref[...])


def _even_mixer(x, g0, g1, w_in, pool_w, pool_scale, lb, out_gain, w_out, layer):
    b, s, d = x.shape
    tile = MIX_TILE
    return pl.pallas_call(
        _even_kernel,
        grid=(b, s // tile),
        in_specs=[
            _tile_spec(tile, d),
            _const_spec(g0.shape), _const_spec(g1.shape), _layer_spec(w_in, layer),
            _const_spec(pool_w.shape), _const_spec(pool_scale.shape), _const_spec(lb.shape),
            _const_spec(out_gain.shape), _layer_spec(w_out, layer),
        ],
        out_specs=_tile_spec(tile, d),
        out_shape=jax.ShapeDtypeStruct(x.shape, x.dtype),
        scratch_shapes=[
            pltpu.VMEM((REC_HEADS, REC_HEAD_DIM, REC_HEAD_DIM), F32),
            pltpu.VMEM((POOL_HALO, POOL_WIDTH), F32),
        ],
        compiler_params=_params(("arbitrary", "arbitrary")),
        name="even_mixer",
    )(x, g0, g1, w_in, pool_w, pool_scale, lb, out_gain, w_out)


def _odd_kernel(x_ref, g0_ref, g1_ref, w_qk_ref, w_vt_ref, bias_ref, w_ot_ref, o_ref, k_ref, vt_ref,
                s_ref):
    tile = x_ref.shape[1]
    d = x_ref.shape[2]
    keys = BAND_ROWS + tile
    n_slab = keys // tile
    j = pl.program_id(1)

    def slab(b):
        return lax.rem(j + b, n_slab)

    @pl.when(j == 0)
    def _():
        k_ref[:n_slab - 1] = jnp.zeros((n_slab - 1, tile, d), BF16)
        vt_ref[:n_slab - 1] = jnp.zeros((n_slab - 1, d, tile), BF16)

    x = x_ref[0]
    h = _rmsnorm(x, g0_ref[...]).astype(BF16)
    qk = _dot(h, w_qk_ref[...])
    q = (qk[:, :d] * (ATT_HEAD_DIM ** -0.5 * LOG2_E)).astype(BF16)
    k_ref[slab(n_slab - 1)] = qk[:, d:].astype(BF16)
    vt_ref[slab(n_slab - 1)] = _dot_nt(w_vt_ref[...], h).astype(BF16)

    low_half = lax.broadcasted_iota(jnp.int32, (1, LANES), 1) < ATT_HEAD_DIM
    ones_rows = jnp.ones((BF16_ROWS, keys), BF16)

    def lanes(head):
        return slice(head // 2 * LANES, (head // 2 + 1) * LANES)

    def score_stage(head):
        own = low_half if head % 2 == 0 else jnp.logical_not(low_half)
        qp = q[:, lanes(head)]
        qm = jnp.where(own, qp, jnp.zeros_like(qp))
        kp = jnp.concatenate([k_ref[slab(b), :, lanes(head)] for b in range(n_slab)], axis=0)
        s = _dot_nt(kp, qm) + bias_ref[head]
        for r in range(0, keys, tile):
            blk = s[r:r + tile, :]
            if r < BAND_ROWS:
                blk = blk + jnp.where(j * tile - BAND_ROWS + r >= 0, 0.0, NEG_BIG)
            s_ref[head % SCORE_SLOTS, r:r + tile, :] = blk

    def value_stage(head):
        s = s_ref[head % SCORE_SLOTS]
        m = jnp.max(s, axis=0, keepdims=True)
        pr = jnp.exp2(s - m).astype(BF16)
        vt = jnp.concatenate([vt_ref[slab(b), lanes(head), :] for b in range(n_slab)], axis=1)
        o = _dot(jnp.concatenate([vt, ones_rows], axis=0), pr)
        return o[:LANES] / o[LANES:LANES + 1]

    outs = []
    ahead = SCORE_SLOTS - 1
    for head in range(ahead):
        score_stage(head)
    for head in range(ATT_HEADS):
        if head + ahead < ATT_HEADS:
            score_stage(head + ahead)
        o = value_stage(head)
        outs.append(o[:ATT_HEAD_DIM] if head % 2 == 0 else o[ATT_HEAD_DIM:])
    att_t = jnp.concatenate(outs, axis=0).astype(BF16)
    yt = _dot(w_ot_ref[...], att_t)
    yt = yt * lax.rsqrt(jnp.mean(yt * yt, axis=0, keepdims=True) + NORM_EPS)
    o_ref[0] = x + yt.T * g1_ref[...]


def _band_bias(rel_bias, tile):
    heads = rel_bias.shape[0]
    band = BAND_ROWS + CHUNK
    n = CHUNK + band - 1
    offs = jnp.arange(n) - (CHUNK - 1)
    idx = jnp.clip(BAND_ROWS - offs, -REL_CLIP, REL_CLIP) + REL_CLIP
    per_offset = jnp.pad(rel_bias[:, idx].astype(F32), ((0, 0), (0, 1)))
    flat = jnp.tile(per_offset, (1, CHUNK + 1))[:, CHUNK - 1:CHUNK - 1 + CHUNK * n]
    block = jnp.swapaxes(flat.reshape(heads, CHUNK, n)[:, :, :band], 1, 2)
    n_chunk = tile // CHUNK
    cols = [jnp.pad(block, ((0, 0), (qc * CHUNK, (n_chunk - 1 - qc) * CHUNK), (0, 0)),
                    constant_values=NEG_BIG) for qc in range(n_chunk)]
    return jnp.concatenate(cols, axis=2)


def _odd_mixer(x, g0, g1, w_qk, w_vt, bias, w_ot):
    b, s, d = x.shape
    tile = MIX_TILE
    return pl.pallas_call(
        _odd_kernel,
        grid=(b, s // tile),
        in_specs=[
            _tile_spec(tile, d),
            _const_spec(g0.shape), _const_spec(g1.shape), _const_spec(w_qk.shape),
            _const_spec(w_vt.shape), _const_spec(bias.shape), _const_spec(w_ot.shape),
        ],
        out_specs=_tile_spec(tile, d),
        out_shape=jax.ShapeDtypeStruct(x.shape, x.dtype),
        scratch_shapes=[
            pltpu.VMEM(((BAND_ROWS + tile) // tile, tile, d), BF16),
            pltpu.VMEM(((BAND_ROWS + tile) // tile, d, tile), BF16),
            pltpu.VMEM((SCORE_SLOTS, BAND_ROWS + tile, tile), F32),
        ],
        compiler_params=_params(("arbitrary", "arbitrary")),
        name="odd_mixer",
    )(x, g0, g1, w_qk, w_vt, bias, w_ot)


def _mem_kv_kernel(mem_ref, g_ref, w_kv_ref, kv_ref):
    b, m, d = mem_ref.shape
    mem_n = _rmsnorm(mem_ref[...].reshape(b * m, d), g_ref[...]).astype(BF16)
    kv_ref[0] = _dot(mem_n, w_kv_ref[0]).astype(BF16).reshape(b, m, 2 * d)


def _mem_kv(mem, gain, w_kv):
    b, m, d = mem.shape
    depth = w_kv.shape[0]
    return pl.pallas_call(
        _mem_kv_kernel,
        grid=(depth,),
        in_specs=[
            pl.BlockSpec((b, m, d), lambda l: (0, 0, 0)),
            pl.BlockSpec((1, d), lambda l: (0, 0)),
            pl.BlockSpec((1, d, 2 * d), lambda l: (l, 0, 0)),
        ],
        out_specs=pl.BlockSpec((1, b, m, 2 * d), lambda l: (l, 0, 0, 0)),
        out_shape=jax.ShapeDtypeStruct((depth, b, m, 2 * d), BF16),
        compiler_params=_params(("arbitrary",)),
        name="mem_kv",
    )(mem, gain, w_kv)


def _dense_kernel(x_ref, g_ref, w_q_ref, kv_ref, w_o_ref, w1_ref, w2_ref, o_ref, s_ref):
    d = x_ref.shape[2]
    dh = d // MEM_HEADS
    rows = x_ref.shape[1] // DENSE_GROUPS
    xs, atts, hs, acts = {}, {}, {}, {}

    def score_stage(r):
        xs[r] = x_ref[0, r * rows:(r + 1) * rows, :]
        h = _rmsnorm(xs[r], g_ref[0:1, :]).astype(BF16)
        q = (_dot(h, w_q_ref[0]) * (dh ** -0.5 * LOG2_E)).astype(BF16)
        for hh in range(MEM_HEADS):
            sl = slice(hh * dh, (hh + 1) * dh)
            s_ref[r, hh] = _dot_nt(q[:, sl], kv_ref[0, 0, :, sl])

    def value_stage(r):
        outs = []
        for hh in range(MEM_HEADS):
            vh = kv_ref[0, 0, :, d + hh * dh:d + (hh + 1) * dh]
            s = s_ref[r, hh]
            m = jnp.max(s, axis=-1, keepdims=True)
            pr = jnp.exp2(s - m)
            l = jnp.sum(pr, axis=-1, keepdims=True)
            outs.append(_dot(pr.astype(BF16), vh) / l)
        atts[r] = jnp.concatenate(outs, axis=-1).astype(BF16)

    def mid_stage(r):
        y = _dot(atts[r], w_o_ref[0])
        xs[r] = xs[r] + _rmsnorm(y, g_ref[1:2, :])
        hs[r] = _rmsnorm(xs[r], g_ref[2:3, :]).astype(BF16)

    def up_stage(r):
        a = jnp.maximum(_dot(hs[r], w1_ref[0]), 0.0)
        acts[r] = (a * a).astype(BF16)

    def down_stage(r):
        y = _dot(acts[r], w2_ref[0])
        o_ref[0, r * rows:(r + 1) * rows, :] = xs[r] + _rmsnorm(y, g_ref[3:4, :])

    stages = (score_stage, value_stage, mid_stage, up_stage, down_stage)
    for step in range(DENSE_GROUPS + len(stages) - 1):
        for depth, stage in enumerate(stages):
            if 0 <= step - depth < DENSE_GROUPS:
                stage(step - depth)


def _dense(x, gains, w_q, kv, w_o, w1, w2, layer, tile):
    b, s, d = x.shape
    m = kv.shape[2]
    return pl.pallas_call(
        _dense_kernel,
        grid=(b, s // tile),
        in_specs=[
            _tile_spec(tile, d),
            _const_spec(gains.shape), _layer_spec(w_q, layer),
            pl.BlockSpec((1, 1, m, 2 * d), lambda i, j: (layer, i, 0, 0)),
            _layer_spec(w_o, layer), _layer_spec(w1, layer), _layer_spec(w2, layer),
        ],
        out_specs=_tile_spec(tile, d),
        out_shape=jax.ShapeDtypeStruct(x.shape, x.dtype),
        scratch_shapes=[pltpu.VMEM((DENSE_GROUPS, MEM_HEADS, tile // DENSE_GROUPS, m), F32)],
        compiler_params=_params(("arbitrary", "arbitrary")),
        name="dense",
    )(x, gains, w_q, kv, w_o, w1, w2)


def kernel(x, mem, norm_gains, mem_norm_gain, w_in_ab, pool_w, pool_scale, rec_lb_logits,
           rec_out_gain, w_out_ab, w_qkv, rel_bias, w_o_att, w_mem_q, w_mem_kv, w_mem_o,
           w_ff1, w_ff2):
    depth = norm_gains.shape[0]
    s = x.shape[1]
    dense_tile = min(DENSE_TILE, s)
    row = lambda a: a.reshape(1, -1)
    lb_p = jax.nn.softmax(rec_lb_logits.astype(F32), axis=0)
    lbs = jnp.clip(jnp.cumsum(lb_p, axis=0) - lb_p[0], 0.0, 1.0)
    w_in_b, w_out_b = _to_bf16(w_in_ab), _to_bf16(w_out_ab)
    w_qkv_b, w_o_b = _to_bf16(w_qkv), _to_bf16(w_o_att)
    w_mem_q_b, w_mem_o_b = _to_bf16(w_mem_q), _to_bf16(w_mem_o)
    w_ff1_b, w_ff2_b = _to_bf16(w_ff1), _to_bf16(w_ff2)
    kv = _mem_kv(mem, row(mem_norm_gain), _to_bf16(w_mem_kv))
    for l in range(depth):
        g = norm_gains[l]
        if l % 2 == 0:
            e = l // 2
            x = _even_mixer(x, row(g[0]), row(g[1]), w_in_b, pool_w[e].astype(BF16),
                            row(pool_scale[e]), row(lbs[e]), row(rec_out_gain[e]), w_out_b, e)
        else:
            o = l // 2
            d = x.shape[2]
            x = _odd_mixer(x, row(g[0]), row(g[1]), w_qkv_b[o, :, :2 * d], w_qkv_b[o, :, 2 * d:].T,
                           _band_bias(LOG2_E * rel_bias[o], MIX_TILE), w_o_b[o].T)
        x = _dense(x, g[2:6], w_mem_q_b, kv, w_mem_o_b, w_ff1_b, w_ff2_b, l, dense_tile)
    return x
```
